```python
import math
import jax, jax.numpy as jnp
from jax import lax
import numpy as np

D_MODEL = 1024
BATCH = 8
SEQ = 2048
DEPTH = 1
DEC_BATCH = 128
DEC_SEQ = 4
PAST_LEN = 2048
PAGE_SIZE = 128

F32 = jnp.float32
HEAD_DIM = 64
A_HEADS = 8
B_HEADS = 4
B_VDIM = 2 * HEAD_DIM
IDX_HEADS = 8
IDX_DIM = 64
TOPK_MAX = 256
N_GROUPS = 4
EXPERTS_PER_GROUP = 8
N_EXPERTS = N_GROUPS * EXPERTS_PER_GROUP
TOP_K_INNER = 2
D_EXPERT = 256
ROPE_THETA = 10000.0
NORM_EPS = 1e-6
Q_BLOCK = 128
MIX_WIDTH = A_HEADS * HEAD_DIM + B_HEADS * B_VDIM
SPLITS = (A_HEADS * HEAD_DIM, A_HEADS * HEAD_DIM, A_HEADS * HEAD_DIM,
          B_HEADS * 2 * HEAD_DIM, B_HEADS * 2 * HEAD_DIM, B_HEADS * B_VDIM,
          IDX_HEADS * IDX_DIM, IDX_DIM, IDX_HEADS)
IN_WIDTH = sum(SPLITS)
SPLIT_POINTS = tuple(int(v) for v in np.cumsum(SPLITS)[:-1])

kernel_name = "hymba_dsa_diffattn_hiermoe_step"


def rms_norm(x, g):
    xf = x.astype(F32)
    y = xf * lax.rsqrt(jnp.mean(xf * xf, axis=-1, keepdims=True) + NORM_EPS)
    return (y * g.astype(F32)).astype(x.dtype)


def rope(x, pos):
    half = x.shape[-1] // 2
    inv = ROPE_THETA ** (-jnp.arange(half, dtype=F32) / half)
    ang = pos.astype(F32)[:, None] * inv[None, :]
    shape = (1, x.shape[1]) + (1,) * (x.ndim - 3) + (half,)
    cos = jnp.cos(ang).reshape(shape)
    sin = jnp.sin(ang).reshape(shape)
    xf = x.astype(F32)
    x1, x2 = xf[..., :half], xf[..., half:]
    return jnp.concatenate([x1 * cos - x2 * sin, x2 * cos + x1 * sin], axis=-1).astype(x.dtype)


def project(xn, w_in, pos, qn_a, kn_a, qn_b, kn_b):
    B, T, _ = xn.shape
    h = xn @ w_in
    aq, ak, av, bq, bk, bv, iq, ik, iw = jnp.split(h, SPLIT_POINTS, axis=-1)
    aq = rope(rms_norm(aq.reshape(B, T, A_HEADS, HEAD_DIM), qn_a), pos)
    ak = rope(rms_norm(ak.reshape(B, T, A_HEADS, HEAD_DIM), kn_a), pos)
    av = av.reshape(B, T, A_HEADS, HEAD_DIM)
    bq = rope(rms_norm(bq.reshape(B, T, B_HEADS, 2, HEAD_DIM), qn_b), pos)
    bk = rope(rms_norm(bk.reshape(B, T, B_HEADS, 2, HEAD_DIM), kn_b), pos)
    bv = bv.reshape(B, T, B_HEADS, B_VDIM)
    iq = rope(iq.reshape(B, T, IDX_HEADS, IDX_DIM), pos)
    ik = rope(ik, pos)
    iw = iw * (IDX_HEADS ** -0.5)
    return aq, ak, av, bq, bk, bv, iq, ik, iw


def sweep_query_blocks(fn, q_args, q_pos):
    T = q_pos.shape[0]
    blk = min(Q_BLOCK, T)
    nb = T // blk

    def split(a):
        return jnp.swapaxes(a.reshape((a.shape[0], nb, blk) + a.shape[2:]), 0, 1)

    out = lax.map(lambda args: fn(args[0], args[1]),
                  (tuple(split(a) for a in q_args), q_pos.reshape(nb, blk)))
    out = jnp.swapaxes(out, 0, 1)
    return out.reshape((out.shape[0], T) + out.shape[3:])


def contiguous_gather_fn(k, v):
    B, S = k.shape[:2]
    kf = k.reshape(B, S, -1)
    vf = v.reshape(B, S, -1)
    take = jax.vmap(lambda a, i: a[i])

    def gather(sel):
        shp = sel.shape + (A_HEADS, HEAD_DIM)
        return take(kf, sel).reshape(shp), take(vf, sel).reshape(shp)
    return gather


def paged_gather_fn(pool_k, pool_v, page_table, k_new, v_new):
    n_pool, ps = pool_k.shape[:2]
    past_len = page_table.shape[1] * ps
    kp = pool_k.reshape(n_pool * ps, -1)
    vp = pool_v.reshape(n_pool * ps, -1)
    B, Tn = k_new.shape[:2]
    kn = k_new.reshape(B, Tn, -1)
    vn = v_new.reshape(B, Tn, -1)
    take = jax.vmap(lambda a, i: a[i])

    def gather(sel):
        in_past = (sel < past_len)[..., None]
        s_past = jnp.minimum(sel, past_len - 1)
        phys = take(page_table, s_past // ps) * ps + s_past % ps
        s_new = jnp.clip(sel - past_len, 0, Tn - 1)
        kg = jnp.where(in_past, kp[phys], take(kn, s_new))
        vg = jnp.where(in_past, vp[phys], take(vn, s_new))
        shp = sel.shape + (A_HEADS, HEAD_DIM)
        return kg.reshape(shp), vg.reshape(shp)
    return gather


def paged_rows(pool, page_table):
    g = pool[page_table]
    return g.reshape((g.shape[0], g.shape[1] * g.shape[2]) + g.shape[3:])


def dsa_attention(q, iq, iw, ik, q_pos, gather_kv):
    L = ik.shape[1]
    topk = min(TOPK_MAX, L // 4)
    k_pos = jnp.arange(L)

    def block(args, qp):
        qb, iqb, iwb = args
        s = jnp.einsum('bqhd,bsd->bqhs', iqb, ik, preferred_element_type=F32) * (IDX_DIM ** -0.5)
        score = jnp.einsum('bqh,bqhs->bqs', iwb.astype(F32), jax.nn.relu(s))
        score = jnp.where(k_pos[None, None, :] <= qp[None, :, None], score, -jnp.inf)
        _, sel = lax.top_k(score, topk)
        kg, vg = gather_kv(sel)
        valid = sel <= qp[None, :, None]
        logits = jnp.einsum('bqhd,bqkhd->bqhk', qb, kg, preferred_element_type=F32) * (HEAD_DIM ** -0.5)
        logits = jnp.where(valid[:, :, None, :], logits, -jnp.inf)
        p = jax.nn.softmax(logits, axis=-1)
        return jnp.einsum('bqhk,bqkhd->bqhd', p.astype(vg.dtype), vg)
    return sweep_query_blocks(block, (q, iq, iw), q_pos)


def diff_attention(q, k, v, q_pos, lam):
    L = k.shape[1]
    k_pos = jnp.arange(L)

    def block(args, qp):
        (qb,) = args
        logits = jnp.einsum('bqhcd,bkhcd->bchqk', qb, k, preferred_element_type=F32) * (HEAD_DIM ** -0.5)
        mask = k_pos[None, None, None, None, :] <= qp[None, None, None, :, None]
        p = jax.nn.softmax(jnp.where(mask, logits, -jnp.inf), axis=-1)
        a = p[:, 0] - lam * p[:, 1]
        return jnp.einsum('bhqk,bkhe->bqhe', a.astype(v.dtype), v)
    return sweep_query_blocks(block, (q,), q_pos)


def hier_moe(x, w_rg, b_rg, w_re, b_re, w_g, w_u, w_d):
    n = x.shape[0]
    g_logits = jnp.einsum('nd,dg->ng', x, w_rg, preferred_element_type=F32) + b_rg.astype(F32)
    g_prob = jax.nn.softmax(g_logits, axis=-1)
    g_sel = jnp.argmax(g_logits, axis=-1)
    g_gate = jnp.take_along_axis(g_prob, g_sel[:, None], axis=-1)
    e_logits = (jnp.einsum('nd,de->ne', x, w_re, preferred_element_type=F32)
                + b_re.astype(F32)).reshape(n, N_GROUPS, EXPERTS_PER_GROUP)
    e_logits = jnp.take_along_axis(e_logits, g_sel[:, None, None], axis=1)[:, 0]
    e_prob = jax.nn.softmax(e_logits, axis=-1)
    top_p, top_i = lax.top_k(e_prob, TOP_K_INNER)
    gates = g_gate * top_p / jnp.sum(top_p, axis=-1, keepdims=True)
    ids = g_sel[:, None] * EXPERTS_PER_GROUP + top_i
    dense_gate = jnp.sum(jax.nn.one_hot(ids, N_EXPERTS, dtype=F32) * gates[..., None], axis=1)
    hg = jnp.einsum('nd,edf->nef', x, w_g)
    hu = jnp.einsum('nd,edf->nef', x, w_u)
    act = jax.nn.silu(hg) * hu * dense_gate[..., None].astype(x.dtype)
    return jnp.einsum('nef,efd->nd', act, w_d)


def merge_and_ffn(x, a_out, b_out, lam_init, subln, w_out, g_ffn, w_rg, b_rg, w_re, b_re, w_g, w_u, w_d):
    B, T, _ = x.shape
    b_out = rms_norm(b_out, subln) * (1.0 - lam_init)
    mixed = jnp.concatenate([a_out.reshape(B, T, -1), b_out.reshape(B, T, -1).astype(a_out.dtype)], axis=-1)
    h = x + mixed @ w_out
    f = hier_moe(rms_norm(h, g_ffn).reshape(B * T, -1), w_rg, b_rg, w_re, b_re, w_g, w_u, w_d)
    return h + f.reshape(B, T, -1)


def setup_inputs(seed: int = 0) -> dict:
    key = jax.random.key(seed)
    ks = jax.random.split(key, 32)
    n_pages = PAST_LEN // PAGE_SIZE
    n_used = DEC_BATCH * n_pages
    n_pool = n_used + max(1, n_used // 4)

    def nrm(k, shape, scale=1.0):
        return jax.random.normal(k, shape, F32) * scale

    def gain(k, shape):
        return 1.0 + 0.01 * jax.random.normal(k, shape, F32)

    page_table = jax.random.permutation(ks[7], n_pool)[:n_used].reshape(DEC_BATCH, n_pages).astype(jnp.int32)
    return {
        "x_prompt": nrm(ks[0], (BATCH, SEQ, D_MODEL)),
        "x_sample": nrm(ks[1], (DEC_BATCH, DEC_SEQ, D_MODEL)),
        "cache_a_k": nrm(ks[2], (DEPTH, n_pool, PAGE_SIZE, A_HEADS, HEAD_DIM)),
        "cache_a_v": nrm(ks[3], (DEPTH, n_pool, PAGE_SIZE, A_HEADS, HEAD_DIM)),
        "cache_idx_k": nrm(ks[4], (DEPTH, n_pool, PAGE_SIZE, IDX_DIM)),
        "cache_b_k": nrm(ks[5], (DEPTH, n_pool, PAGE_SIZE, B_HEADS, 2, HEAD_DIM)),
        "cache_b_v": nrm(ks[6], (DEPTH, n_pool, PAGE_SIZE, B_HEADS, B_VDIM)),
        "page_table": page_table,
        "g_mix": gain(ks[8], (DEPTH, D_MODEL)),
        "w_in": nrm(ks[9], (DEPTH, D_MODEL, IN_WIDTH), D_MODEL ** -0.5),
        "q_norm_a": gain(ks[10], (DEPTH, HEAD_DIM)),
        "k_norm_a": gain(ks[11], (DEPTH, HEAD_DIM)),
        "q_norm_b": gain(ks[12], (DEPTH, HEAD_DIM)),
        "k_norm_b": gain(ks[13], (DEPTH, HEAD_DIM)),
        "lambda_q1": nrm(ks[14], (DEPTH, HEAD_DIM), 0.1),
        "lambda_k1": nrm(ks[15], (DEPTH, HEAD_DIM), 0.1),
        "lambda_q2": nrm(ks[16], (DEPTH, HEAD_DIM), 0.1),
        "lambda_k2": nrm(ks[17], (DEPTH, HEAD_DIM), 0.1),
        "subln_b": gain(ks[18], (DEPTH, B_VDIM)),
        "w_out": nrm(ks[19], (DEPTH, MIX_WIDTH, D_MODEL), MIX_WIDTH ** -0.5),
        "g_ffn": gain(ks[20], (DEPTH, D_MODEL)),
        "w_router_group": nrm(ks[21], (DEPTH, D_MODEL, N_GROUPS), D_MODEL ** -0.5),
        "b_router_group": nrm(ks[22], (DEPTH, N_GROUPS), 0.01),
        "w_router_expert": nrm(ks[23], (DEPTH, D_MODEL, N_EXPERTS), D_MODEL ** -0.5),
        "b_router_expert": nrm(ks[24], (DEPTH, N_EXPERTS), 0.01),
        "w_exp_gate": nrm(ks[25], (DEPTH, N_EXPERTS, D_MODEL, D_EXPERT), D_MODEL ** -0.5),
        "w_exp_up": nrm(ks[26], (DEPTH, N_EXPERTS, D_MODEL, D_EXPERT), D_MODEL ** -0.5),
        "w_exp_down": nrm(ks[27], (DEPTH, N_EXPERTS, D_EXPERT, D_MODEL), D_EXPERT ** -0.5),
    }


def reference(x_prompt, x_sample, cache_a_k, cache_a_v, cache_idx_k, cache_b_k, cache_b_v, page_table,
              g_mix, w_in, q_norm_a, k_norm_a, q_norm_b, k_norm_b, lambda_q1, lambda_k1, lambda_q2, lambda_k2,
              subln_b, w_out, g_ffn, w_router_group, b_router_group, w_router_expert, b_router_expert,
              w_exp_gate, w_exp_up, w_exp_down):
    hp, hs = x_prompt, x_sample
    past_len = page_table.shape[1] * PAGE_SIZE
    pos_p = jnp.arange(hp.shape[1])
    pos_s = past_len + jnp.arange(hs.shape[1])
    ak_p, av_p, ik_p, bk_p, bv_p = [], [], [], [], []
    ak_s, av_s, ik_s, bk_s, bv_s = [], [], [], [], []
    for l in range(DEPTH):
        lam_init = 0.8 - 0.6 * math.exp(-0.3 * l)
        lam = (jnp.exp(jnp.sum(lambda_q1[l].astype(F32) * lambda_k1[l].astype(F32)))
               - jnp.exp(jnp.sum(lambda_q2[l].astype(F32) * lambda_k2[l].astype(F32))) + lam_init)
        ffn_args = (lam_init, subln_b[l], w_out[l], g_ffn[l], w_router_group[l], b_router_group[l],
                    w_router_expert[l], b_router_expert[l], w_exp_gate[l], w_exp_up[l], w_exp_down[l])
        aq, ak, av, bq, bk, bv, iq, ik, iw = project(rms_norm(hp, g_mix[l]), w_in[l], pos_p,
                                                     q_norm_a[l], k_norm_a[l], q_norm_b[l], k_norm_b[l])
        a_out = dsa_attention(aq, iq, iw, ik, pos_p, contiguous_gather_fn(ak, av))
        b_out = diff_attention(bq, bk, bv, pos_p, lam)
        hp = merge_and_ffn(hp, a_out, b_out, *ffn_args)
        ak_p.append(ak); av_p.append(av); ik_p.append(ik); bk_p.append(bk); bv_p.append(bv)
        aq, ak, av, bq, bk, bv, iq, ik, iw = project(rms_norm(hs, g_mix[l]), w_in[l], pos_s,
                                                     q_norm_a[l], k_norm_a[l], q_norm_b[l], k_norm_b[l])
        ik_all = jnp.concatenate([paged_rows(cache_idx_k[l], page_table).astype(ik.dtype), ik], axis=1)
        a_out = dsa_attention(aq, iq, iw, ik_all, pos_s,
                              paged_gather_fn(cache_a_k[l].astype(ak.dtype), cache_a_v[l].astype(av.dtype), page_table, ak, av))
        bk_all = jnp.concatenate([paged_rows(cache_b_k[l], page_table).astype(bk.dtype), bk], axis=1)
        bv_all = jnp.concatenate([paged_rows(cache_b_v[l], page_table).astype(bv.dtype), bv], axis=1)
        b_out = diff_attention(bq, bk_all, bv_all, pos_s, lam)
        hs = merge_and_ffn(hs, a_out, b_out, *ffn_args)
        ak_s.append(ak); av_s.append(av); ik_s.append(ik); bk_s.append(bk); bv_s.append(bv)
    return (hp, hs,
            jnp.stack(ak_p), jnp.stack(av_p), jnp.stack(ik_p), jnp.stack(bk_p), jnp.stack(bv_p),
            jnp.stack(ak_s), jnp.stack(av_s), jnp.stack(ik_s), jnp.stack(bk_s), jnp.stack(bv_s))
```

```python
import functools
import math

import jax
import jax.numpy as jnp
import numpy as np
from jax import lax
from jax.experimental import pallas as pl
from jax.experimental.pallas import tpu as pltpu

F32 = jnp.float32
BF16 = jnp.bfloat16
I32 = jnp.int32

D_MODEL = 1024
HEAD_DIM = 64
LANES = 128
SEC = 512
N_SEC = 7
PAGE = 128
TOPK = 256
N_GROUPS = 4
EXPERTS_PER_GROUP = 8
N_EXPERTS = N_GROUPS * EXPERTS_PER_GROUP
D_EXPERT = 256
ROPE_THETA = 10000.0
NORM_EPS = 1e-6
NEG = -1e30
INT_MIN = -2 ** 31
VMEM_LIMIT = 56 * 1024 * 1024


def _dot(a, b):
    return jnp.dot(a, b, preferred_element_type=F32)


def _dot_nt(a, b):
    return lax.dot_general(a, b, (((1,), (1,)), ((), ())), preferred_element_type=F32)


def _lane_iota(shape):
    return lax.broadcasted_iota(I32, shape, len(shape) - 1)


def _row_iota(shape):
    return lax.broadcasted_iota(I32, shape, len(shape) - 2)


def _proj_kernel(x_ref, g_ref, wm_ref, wt_ref, gains_ref, cos_ref, sin_ref, gmat_ref,
                 akf_ref, avf_ref, bkf_ref, bvf_ref, ikf_ref, iwf_ref,
                 aqb_ref, akb_ref, avb_ref, bqb_ref, bkb_ref, bvb_ref, iqb_ref, ikb_ref):
    xf = x_ref[...]
    ms = jnp.mean(xf * xf, axis=-1, keepdims=True)
    xn = (xf * lax.rsqrt(ms + NORM_EPS) * g_ref[...]).astype(BF16)
    cos = cos_ref[...]
    sin = sin_ref[...]
    tm = xf.shape[0]
    first_half = (_lane_iota((tm, LANES)) % HEAD_DIM) < (HEAD_DIM // 2)
    gmat = gmat_ref[...]

    def rope(c):
        swapped = jnp.where(first_half, pltpu.roll(c, LANES - HEAD_DIM // 2, 1),
                            pltpu.roll(c, HEAD_DIM // 2, 1))
        return c * cos + swapped * sin

    def headnorm(c, gain):
        ssq = _dot((c * c).astype(BF16), gmat)
        return c * lax.rsqrt(ssq * (1.0 / HEAD_DIM) + NORM_EPS) * gain

    plan = ((0, True, None, aqb_ref), (1, True, akf_ref, akb_ref), (None, False, avf_ref, avb_ref),
            (2, True, None, bqb_ref), (3, True, bkf_ref, bkb_ref), (None, False, bvf_ref, bvb_ref),
            (None, True, None, iqb_ref))
    for s, (gi, do_rope, f_ref, b_ref) in enumerate(plan):
        hs = _dot(xn, wm_ref[:, s * SEC:(s + 1) * SEC])
        for c in range(SEC // LANES):
            ch = hs[:, c * LANES:(c + 1) * LANES]
            if gi is not None:
                ch = headnorm(ch, gains_ref[gi:gi + 1, :])
            if do_rope:
                ch = rope(ch)
            if f_ref is not None:
                f_ref[:, c * LANES:(c + 1) * LANES] = ch
            b_ref[:, c * LANES:(c + 1) * LANES] = ch.astype(BF16)
    ht = _dot(xn, wt_ref[...])
    ik2 = rope(ht[:, :LANES])
    ikf_ref[...] = ik2
    ikb_ref[...] = ik2.astype(BF16)
    iwf_ref[...] = ht[:, LANES:] * (8 ** -0.5)


def _project(x2d, consts, tm, table_rows_per_period):
    n = x2d.shape[0]
    nper = table_rows_per_period // tm
    grid = (n // tm,)
    row = lambda i: (i, 0)
    fixed = lambda i: (0, 0)
    tab = (lambda i: (i % nper, 0)) if nper > 1 else fixed
    wide_f = jax.ShapeDtypeStruct((n, SEC), F32)
    wide_b = jax.ShapeDtypeStruct((n, SEC), BF16)
    lane_f = jax.ShapeDtypeStruct((n, LANES), F32)
    lane_b = jax.ShapeDtypeStruct((n, LANES), BF16)
    out_shape = (wide_f, wide_f, wide_f, wide_f, lane_f, lane_f,
                 wide_b, wide_b, wide_b, wide_b, wide_b, wide_b, wide_b, lane_b)
    wide_spec = pl.BlockSpec((tm, SEC), row)
    lane_spec = pl.BlockSpec((tm, LANES), row)
    out_specs = (wide_spec,) * 4 + (lane_spec,) * 2 + (wide_spec,) * 7 + (lane_spec,)
    return pl.pallas_call(
        _proj_kernel,
        grid=grid,
        in_specs=[
            pl.BlockSpec((tm, D_MODEL), row),
            pl.BlockSpec((1, D_MODEL), fixed),
            pl.BlockSpec((D_MODEL, N_SEC * SEC), fixed),
            pl.BlockSpec((D_MODEL, 2 * LANES), fixed),
            pl.BlockSpec((4, LANES), fixed),
            pl.BlockSpec((tm, LANES), tab),
            pl.BlockSpec((tm, LANES), tab),
            pl.BlockSpec((LANES, LANES), fixed),
        ],
        out_specs=out_specs,
        out_shape=out_shape,
        compiler_params=pltpu.CompilerParams(dimension_semantics=("arbitrary",),
                                             vmem_limit_bytes=VMEM_LIMIT),
        name="project",
    )(x2d, consts["g_mix"], consts["w_main"], consts["w_tail"], consts["gains"],
      consts["cos"], consts["sin"], consts["gmat"])


def _topk_mask(score, key_ref, k, idx_bits):
    r, l = score.shape
    bits = pltpu.bitcast(score + 0.0, I32)
    key_ref[...] = jnp.where(bits < 0, bits ^ jnp.int32(0x7FFFFFFF), bits)
    kf = jnp.float32(k)

    def count_ge(cand):
        return jnp.sum(jnp.where(key_ref[...] >= cand, 1.0, 0.0), axis=1, keepdims=True)

    t0 = jnp.where(count_ge(jnp.zeros((r, 1), I32)) >= kf, jnp.int32(0), jnp.int32(INT_MIN))

    def value_step(i, t):
        cand = t | jnp.left_shift(jnp.int32(1), jnp.int32(30) - i)
        return jnp.where(count_ge(cand) >= kf, cand, t)

    thr = lax.fori_loop(0, 31, value_step, t0)
    keys = key_ref[...]
    gt = keys > thr
    need = kf - jnp.sum(jnp.where(gt, 1.0, 0.0), axis=1, keepdims=True)
    idx = _lane_iota((r, l))

    def index_step(i, j0):
        cand = j0 | jnp.left_shift(jnp.int32(1), jnp.int32(idx_bits - 1) - i)
        cnt = jnp.sum(jnp.where((key_ref[...] == thr) & (idx < cand), 1.0, 0.0), axis=1, keepdims=True)
        return jnp.where(cnt < need, cand, j0)

    j0 = lax.fori_loop(0, idx_bits, index_step, jnp.zeros((r, 1), I32))
    return gt | ((keys == thr) & (idx <= j0))


def _dsa_kernel(iq_ref, iw_ref, aq_ref, ik_ref, ak_ref, av_ref, out_ref, key_ref):
    tq = iq_ref.shape[0]
    l = ik_ref.shape[0]
    qi = pl.program_id(1)
    low = _lane_iota((tq, LANES)) < HEAD_DIM
    ik2 = ik_ref[...]
    score = jnp.zeros((tq, l), F32)
    for j in range(SEC // LANES):
        iqc = iq_ref[:, j * LANES:(j + 1) * LANES]
        for par in range(2):
            h = 2 * j + par
            lhs = jnp.where(low if par == 0 else ~low, iqc, jnp.zeros_like(iqc))
            s = _dot_nt(lhs, ik2)
            w = iw_ref[:, h:h + 1] * 0.125
            score = score + w * jnp.maximum(s, 0.0)
    qpos = qi * tq + _row_iota((tq, 1))
    kpos = _lane_iota((tq, l))
    causal = kpos <= qpos
    score = jnp.where(causal, score, -jnp.inf)
    mask = _topk_mask(score, key_ref, TOPK, 12) & causal
    for j in range(SEC // LANES):
        aqc = aq_ref[:, j * LANES:(j + 1) * LANES]
        kc = ak_ref[:, j * LANES:(j + 1) * LANES]
        vc = av_ref[:, j * LANES:(j + 1) * LANES]
        outs = []
        for par in range(2):
            lhs = jnp.where(low if par == 0 else ~low, aqc, jnp.zeros_like(aqc))
            logits = jnp.where(mask, _dot_nt(lhs, kc) * 0.125, NEG)
            m = jnp.max(logits, axis=1, keepdims=True)
            p = jnp.exp(logits - m)
            den = jnp.sum(p, axis=1, keepdims=True)
            outs.append(_dot(p.astype(BF16), vc) / den)
        out_ref[:, j * LANES:(j + 1) * LANES] = jnp.where(low, outs[0], outs[1])


def _dsa_prompt(iq, iw, aq, ik2, ak, av, batch, seq, tq):
    nq = seq // tq
    qrow = lambda b, q: (b * nq + q, 0)
    krow = lambda b, q: (b, 0)
    return pl.pallas_call(
        _dsa_kernel,
        grid=(batch, nq),
        in_specs=[
            pl.BlockSpec((tq, SEC), qrow),
            pl.BlockSpec((tq, LANES), qrow),
            pl.BlockSpec((tq, SEC), qrow),
            pl.BlockSpec((seq, LANES), krow),
            pl.BlockSpec((seq, SEC), krow),
            pl.BlockSpec((seq, SEC), krow),
        ],
        out_specs=pl.BlockSpec((tq, SEC), qrow),
        out_shape=jax.ShapeDtypeStruct((batch * seq, SEC), F32),
        scratch_shapes=[pltpu.VMEM((tq, seq), I32)],
        compiler_params=pltpu.CompilerParams(dimension_semantics=("arbitrary", "arbitrary"),
                                             vmem_limit_bytes=VMEM_LIMIT),
        name="dsa_prompt",
    )(iq, iw, aq, ik2, ak, av)


def _lambda_value(lam_ref, lam_init):
    lq1, lk1, lq2, lk2 = (lam_ref[i:i + 1, :] for i in range(4))
    s1 = jnp.sum(lq1 * lk1, axis=1, keepdims=True)
    s2 = jnp.sum(lq2 * lk2, axis=1, keepdims=True)
    return jnp.exp(s1) - jnp.exp(s2) + lam_init


def _subln(o, gain, lam_init):
    ms = jnp.mean(o * o, axis=-1, keepdims=True)
    return o * lax.rsqrt(ms + NORM_EPS) * gain * (1.0 - lam_init)


def _diff_kernel(bq_ref, bk_ref, bv_ref, lam_ref, subln_ref, out_ref, *, lam_init):
    tq = bq_ref.shape[0]
    l = bk_ref.shape[0]
    qi = pl.program_id(1)
    low = _lane_iota((tq, LANES)) < HEAD_DIM
    lam = _lambda_value(lam_ref, lam_init)
    causal = _lane_iota((tq, l)) <= qi * tq + _row_iota((tq, 1))
    for h in range(SEC // LANES):
        qc = bq_ref[:, h * LANES:(h + 1) * LANES]
        kc = bk_ref[:, h * LANES:(h + 1) * LANES]
        vc = bv_ref[:, h * LANES:(h + 1) * LANES]
        probs = []
        for c in range(2):
            lhs = jnp.where(low if c == 0 else ~low, qc, jnp.zeros_like(qc))
            logits = jnp.where(causal, _dot_nt(lhs, kc) * 0.125, NEG)
            m = jnp.max(logits, axis=1, keepdims=True)
            p = jnp.exp(logits - m)
            probs.append(p / jnp.sum(p, axis=1, keepdims=True))
        a = probs[0] - lam * probs[1]
        o = _dot(a.astype(BF16), vc)
        out_ref[:, h * LANES:(h + 1) * LANES] = _subln(o, subln_ref[...], lam_init)


def _diff_prompt(bq, bk, bv, lam_vecs, subln, batch, seq, tq, lam_init):
    nq = seq // tq
    qrow = lambda b, q: (b * nq + q, 0)
    krow = lambda b, q: (b, 0)
    fixed = lambda b, q: (0, 0)
    return pl.pallas_call(
        functools.partial(_diff_kernel, lam_init=lam_init),
        grid=(batch, nq),
        in_specs=[
            pl.BlockSpec((tq, SEC), qrow),
            pl.BlockSpec((seq, SEC), krow),
            pl.BlockSpec((seq, SEC), krow),
            pl.BlockSpec((4, HEAD_DIM), fixed),
            pl.BlockSpec((1, LANES), fixed),
        ],
        out_specs=pl.BlockSpec((tq, SEC), qrow),
        out_shape=jax.ShapeDtypeStruct((batch * seq, SEC), F32),
        compiler_params=pltpu.CompilerParams(dimension_semantics=("arbitrary", "arbitrary"),
                                             vmem_limit_bytes=VMEM_LIMIT),
        name="diff_prompt",
    )(bq, bk, bv, lam_vecs, subln)


def _sample_kernel(pt_ref, iqr_ref, iwr_ref, aq_ref, bq_ref, ikn_ref, akn_ref, avn_ref, bkn_ref, bvn_ref,
                   lam_ref, subln_ref, *rest, n_pages, n_new, lam_init):
    del pt_ref
    pages = rest[:5 * n_pages]
    idx_pages = pages[0 * n_pages:1 * n_pages]
    ak_pages = pages[1 * n_pages:2 * n_pages]
    av_pages = pages[2 * n_pages:3 * n_pages]
    bk_pages = pages[3 * n_pages:4 * n_pages]
    bv_pages = pages[4 * n_pages:5 * n_pages]
    aout_ref, bout_ref, key_ref, new_ref = rest[5 * n_pages:]
    past = n_pages * PAGE
    ltot = past + PAGE
    rows = 8 * n_new
    lam = _lambda_value(lam_ref, lam_init)

    def padded_new(src_ref):
        width = src_ref.shape[-1]
        new_ref[:, :width] = jnp.zeros((PAGE, width), F32)
        new_ref[0:n_new, :width] = src_ref[0]
        return new_ref[:, :width].astype(BF16)

    lane = _lane_iota((8, ltot))
    newpos = lane - past
    iq = iqr_ref[0].astype(BF16)
    parts = [_dot_nt(iq, idx_pages[p][0, 0].astype(BF16)) for p in range(n_pages)]
    parts.append(_dot_nt(iq, padded_new(ikn_ref)))
    s = jnp.concatenate(parts, axis=1)
    s = jnp.maximum(s, 0.0) * (iwr_ref[0] * 0.125)
    rowid = _row_iota((8, ltot))
    score = jnp.full((8, ltot), -jnp.inf, F32)
    causal8 = jnp.zeros((8, ltot), jnp.bool_)
    for i in range(n_new):
        si = jnp.sum(s[8 * i:8 * (i + 1), :], axis=0, keepdims=True)
        ok = (lane < past) | ((newpos >= 0) & (newpos <= i))
        score = jnp.where((rowid == i) & ok, jnp.broadcast_to(si, (8, ltot)), score)
        causal8 = causal8 | ((rowid == i) & ok)
    sel8 = _topk_mask(score, key_ref, TOPK, 12) & causal8
    sel8f = jnp.where(sel8, 1.0, 0.0)
    causal8f = jnp.where(causal8, 1.0, 0.0)

    def expand(m8f):
        return jnp.concatenate(
            [jnp.broadcast_to(m8f[i:i + 1, :], (8, ltot)) for i in range(n_new)], axis=0) > 0.5

    lane_s = _lane_iota((8, SEC))
    row_s = _row_iota((8, SEC))
    slot_mask = (lane_s // HEAD_DIM) == row_s

    def block_diag(q_ref):
        q = q_ref[0]
        blocks = [jnp.where(slot_mask, jnp.broadcast_to(q[i:i + 1, :], (8, SEC)), 0.0)
                  for i in range(n_new)]
        return jnp.concatenate(blocks, axis=0).astype(BF16)

    def attend(q_ref, k_pages, kn_ref, v_pages, vn_ref, mask):
        qbd = block_diag(q_ref)
        lparts = [_dot_nt(qbd, k_pages[p][0, 0].astype(BF16)) for p in range(n_pages)]
        lparts.append(_dot_nt(qbd, padded_new(kn_ref)))
        logits = jnp.where(mask, jnp.concatenate(lparts, axis=1) * 0.125, NEG)
        m = jnp.max(logits, axis=1, keepdims=True)
        p = jnp.exp(logits - m)
        den = jnp.sum(p, axis=1, keepdims=True)
        pb = p.astype(BF16)
        o = _dot(pb[:, past:], padded_new(vn_ref))
        for pg in range(n_pages):
            o = o + _dot(pb[:, pg * PAGE:(pg + 1) * PAGE], v_pages[pg][0, 0].astype(BF16))
        return o / den

    oa = attend(aq_ref, ak_pages, akn_ref, av_pages, avn_ref, expand(sel8f))
    for i in range(n_new):
        blk = jnp.where(slot_mask, oa[8 * i:8 * (i + 1), :], 0.0)
        aout_ref[0, i:i + 1, :] = jnp.sum(blk, axis=0, keepdims=True)

    ob = attend(bq_ref, bk_pages, bkn_ref, bv_pages, bvn_ref, expand(causal8f))
    pair_mask = (lane_s // LANES) == (row_s // 2)
    sign = jnp.where(row_s % 2 == 0, 1.0, -lam)
    for i in range(n_new):
        blk = jnp.where(pair_mask, ob[8 * i:8 * (i + 1), :] * sign, 0.0)
        orow = jnp.sum(blk, axis=0, keepdims=True)
        for h in range(SEC // LANES):
            seg = orow[:, h * LANES:(h + 1) * LANES]
            bout_ref[0, i:i + 1, h * LANES:(h + 1) * LANES] = _subln(seg, subln_ref[...], lam_init)


def _sample_attention(page_table, iq_r, iw_r, aq, bq, ik_new, ak_new, av_new, bk_new, bv_new,
                      lam_vecs, subln, cache_idx, cache_ak, cache_av, cache_bk, cache_bv, lam_init):
    nb, n_pages = page_table.shape
    n_new = aq.shape[1]
    rows = 8 * n_new
    ltot = n_pages * PAGE + PAGE
    seq3 = lambda b, pt: (b, 0, 0)
    fixed = lambda b, pt: (0, 0)

    def page_spec(width, p):
        return pl.BlockSpec((1, 1, PAGE, width), lambda b, pt, p=p: (0, pt[b, p], 0, 0))

    in_specs = [
        pl.BlockSpec((1, rows, HEAD_DIM), seq3),
        pl.BlockSpec((1, rows, 1), seq3),
        pl.BlockSpec((1, n_new, SEC), seq3),
        pl.BlockSpec((1, n_new, SEC), seq3),
        pl.BlockSpec((1, n_new, HEAD_DIM), seq3),
        pl.BlockSpec((1, n_new, SEC), seq3),
        pl.BlockSpec((1, n_new, SEC), seq3),
        pl.BlockSpec((1, n_new, SEC), seq3),
        pl.BlockSpec((1, n_new, SEC), seq3),
        pl.BlockSpec((4, HEAD_DIM), fixed),
        pl.BlockSpec((1, LANES), fixed),
    ]
    operands = [iq_r, iw_r, aq, bq, ik_new, ak_new, av_new, bk_new, bv_new, lam_vecs, subln]
    for cache, width in ((cache_idx, HEAD_DIM), (cache_ak, SEC), (cache_av, SEC),
                         (cache_bk, SEC), (cache_bv, SEC)):
        for p in range(n_pages):
            in_specs.append(page_spec(width, p))
            operands.append(cache)
    out_spec = pl.BlockSpec((1, n_new, SEC), seq3)
    grid_spec = pltpu.PrefetchScalarGridSpec(
        num_scalar_prefetch=1,
        grid=(nb,),
        in_specs=in_specs,
        out_specs=(out_spec, out_spec),
        scratch_shapes=[pltpu.VMEM((8, ltot), I32), pltpu.VMEM((PAGE, SEC), F32)],
    )
    return pl.pallas_call(
        functools.partial(_sample_kernel, n_pages=n_pages, n_new=n_new, lam_init=lam_init),
        grid_spec=grid_spec,
        out_shape=(jax.ShapeDtypeStruct((nb, n_new, SEC), F32),) * 2,
        compiler_params=pltpu.CompilerParams(dimension_semantics=("arbitrary",),
                                             vmem_limit_bytes=VMEM_LIMIT),
        name="sample_attention",
    )(page_table, *operands)


def _merge_kernel(x_ref, a_ref, b_ref, woa_ref, wob_ref, g_ref, wrh_ref, wrl_ref, br_ref,
                  h_ref, hn_ref, gate_ref):
    h = (x_ref[...] + _dot(a_ref[...].astype(BF16), woa_ref[...])
         + _dot(b_ref[...].astype(BF16), wob_ref[...]))
    h_ref[...] = h
    ms = jnp.mean(h * h, axis=-1, keepdims=True)
    hn = h * lax.rsqrt(ms + NORM_EPS) * g_ref[...]
    hi = hn.astype(BF16)
    lo = (hn - hi.astype(F32)).astype(BF16)
    hn_ref[...] = hi
    logits = (_dot(hi, wrh_ref[...]) + _dot(lo, wrh_ref[...]) + _dot(hi, wrl_ref[...])) + br_ref[...]
    tm = logits.shape[0]
    lane = _lane_iota((tm, LANES))
    big = jnp.int32(LANES)
    is_g = lane < N_GROUPS
    gl = jnp.where(is_g, logits, -jnp.inf)
    gmax = jnp.max(gl, axis=1, keepdims=True)
    g_sel = jnp.min(jnp.where(is_g & (gl == gmax), lane, big), axis=1, keepdims=True)
    g_gate = 1.0 / jnp.sum(jnp.where(is_g, jnp.exp(gl - gmax), 0.0), axis=1, keepdims=True)
    e_idx = lane - N_GROUPS
    in_grp = (e_idx >= g_sel * EXPERTS_PER_GROUP) & (e_idx < (g_sel + 1) * EXPERTS_PER_GROUP)
    el = jnp.where(in_grp, logits, -jnp.inf)
    e1 = jnp.max(el, axis=1, keepdims=True)
    i1 = jnp.min(jnp.where(in_grp & (el == e1), lane, big), axis=1, keepdims=True)
    el2 = jnp.where(lane == i1, -jnp.inf, el)
    e2 = jnp.max(el2, axis=1, keepdims=True)
    i2 = jnp.min(jnp.where(in_grp & (el2 == e2), lane, big), axis=1, keepdims=True)
    den = jnp.sum(jnp.where(in_grp, jnp.exp(el - e1), 0.0), axis=1, keepdims=True)
    p1 = 1.0 / den
    p2 = jnp.exp(e2 - e1) / den
    gate1 = g_gate * p1 / (p1 + p2)
    gate2 = g_gate * p2 / (p1 + p2)
    gate_ref[...] = (jnp.where(lane + N_GROUPS == i1, gate1, 0.0)
                     + jnp.where(lane + N_GROUPS == i2, gate2, 0.0))


def _merge(x2d, a_out, b_out, consts, tm):
    n = x2d.shape[0]
    row = lambda i: (i, 0)
    fixed = lambda i: (0, 0)
    return pl.pallas_call(
        _merge_kernel,
        grid=(n // tm,),
        in_specs=[
            pl.BlockSpec((tm, D_MODEL), row),
            pl.BlockSpec((tm, SEC), row),
            pl.BlockSpec((tm, SEC), row),
            pl.BlockSpec((SEC, D_MODEL), fixed),
            pl.BlockSpec((SEC, D_MODEL), fixed),
            pl.BlockSpec((1, D_MODEL), fixed),
            pl.BlockSpec((D_MODEL, LANES), fixed),
            pl.BlockSpec((D_MODEL, LANES), fixed),
            pl.BlockSpec((1, LANES), fixed),
        ],
        out_specs=(pl.BlockSpec((tm, D_MODEL), row), pl.BlockSpec((tm, D_MODEL), row),
                   pl.BlockSpec((tm, LANES), row)),
        out_shape=(jax.ShapeDtypeStruct((n, D_MODEL), F32), jax.ShapeDtypeStruct((n, D_MODEL), BF16),
                   jax.ShapeDtypeStruct((n, LANES), F32)),
        compiler_params=pltpu.CompilerParams(dimension_semantics=("arbitrary",),
                                             vmem_limit_bytes=VMEM_LIMIT),
        name="merge_router",
    )(x2d, a_out, b_out, consts["w_out_a"], consts["w_out_b"], consts["g_ffn"],
      consts["w_r_hi"], consts["w_r_lo"], consts["b_r"])


def _moe_kernel(hn_ref, gate_ref, h_ref, wg_ref, wu_ref, wd_ref, y_ref, acc_ref):
    e = pl.program_id(1)

    @pl.when(e == 0)
    def _():
        acc_ref[...] = jnp.zeros_like(acc_ref)

    hn = hn_ref[...]
    gates = gate_ref[...]
    lane = _lane_iota(gates.shape)
    ge = jnp.sum(jnp.where(lane == e, gates, 0.0), axis=1, keepdims=True)
    hg = _dot(hn, wg_ref[0])
    hu = _dot(hn, wu_ref[0])
    act = (hg * jax.nn.sigmoid(hg)) * hu * ge
    acc_ref[...] += _dot(act.astype(BF16), wd_ref[0])

    @pl.when(e == pl.num_programs(1) - 1)
    def _():
        y_ref[...] = h_ref[...] + acc_ref[...]


def _moe(hn, gates, h, consts, tm):
    n = hn.shape[0]
    row = lambda i, e: (i, 0)
    return pl.pallas_call(
        _moe_kernel,
        grid=(n // tm, N_EXPERTS),
        in_specs=[
            pl.BlockSpec((tm, D_MODEL), row),
            pl.BlockSpec((tm, LANES), row),
            pl.BlockSpec((tm, D_MODEL), row),
            pl.BlockSpec((1, D_MODEL, D_EXPERT), lambda i, e: (e, 0, 0)),
            pl.BlockSpec((1, D_MODEL, D_EXPERT), lambda i, e: (e, 0, 0)),
            pl.BlockSpec((1, D_EXPERT, D_MODEL), lambda i, e: (e, 0, 0)),
        ],
        out_specs=pl.BlockSpec((tm, D_MODEL), row),
        out_shape=jax.ShapeDtypeStruct((n, D_MODEL), F32),
        scratch_shapes=[pltpu.VMEM((tm, D_MODEL), F32)],
        compiler_params=pltpu.CompilerParams(dimension_semantics=("arbitrary", "arbitrary"),
                                             vmem_limit_bytes=VMEM_LIMIT),
        name="moe_experts",
    )(hn, gates, h, consts["w_g"], consts["w_u"], consts["w_d"])


def _rope_tables(positions):
    half = HEAD_DIM // 2
    inv = ROPE_THETA ** (-jnp.arange(half, dtype=F32) / half)
    ang = positions.astype(F32)[:, None] * inv[None, :]
    cos, sin = jnp.cos(ang), jnp.sin(ang)
    cos64 = jnp.concatenate([cos, cos], axis=1)
    sin64 = jnp.concatenate([-sin, sin], axis=1)
    return jnp.tile(cos64, (1, LANES // HEAD_DIM)), jnp.tile(sin64, (1, LANES // HEAD_DIM))


def _tile_lanes(v):
    return jnp.tile(v.astype(F32).reshape(1, -1), (1, LANES // v.shape[-1]))


def kernel(x_prompt, x_sample, cache_a_k, cache_a_v, cache_idx_k, cache_b_k, cache_b_v, page_table,
           g_mix, w_in, q_norm_a, k_norm_a, q_norm_b, k_norm_b, lambda_q1, lambda_k1, lambda_q2, lambda_k2,
           subln_b, w_out, g_ffn, w_router_group, b_router_group, w_router_expert, b_router_expert,
           w_exp_gate, w_exp_up, w_exp_down):
    depth = w_in.shape[0]
    assert depth == 1, "single-layer stack"
    batch, seq, _ = x_prompt.shape
    nb, n_new, _ = x_sample.shape
    n_pages = page_table.shape[1]
    past = n_pages * PAGE
    l = 0
    lam_init = 0.8 - 0.6 * math.exp(-0.3 * l)

    wide = N_SEC * SEC
    w = w_in[l]
    w_ik = w[:, wide:wide + HEAD_DIM]
    w_iw = w[:, wide + HEAD_DIM:]
    w_tail = jnp.concatenate(
        [w_ik, w_ik, w_iw, jnp.zeros((D_MODEL, LANES - w_iw.shape[1]), F32)], axis=1).astype(BF16)
    gmat = (np.arange(LANES)[:, None] // HEAD_DIM == np.arange(LANES)[None, :] // HEAD_DIM)
    w_r = jnp.concatenate([w_router_group[l], w_router_expert[l],
                           jnp.zeros((D_MODEL, LANES - N_GROUPS - N_EXPERTS), F32)], axis=1)
    w_r_hi = w_r.astype(BF16)
    b_r = jnp.concatenate([b_router_group[l], b_router_expert[l],
                           jnp.zeros((LANES - N_GROUPS - N_EXPERTS,), F32)]).reshape(1, LANES)
    consts = {
        "g_mix": g_mix[l].reshape(1, D_MODEL),
        "w_main": w[:, :wide].astype(BF16),
        "w_tail": w_tail,
        "gains": jnp.concatenate([_tile_lanes(q_norm_a[l]), _tile_lanes(k_norm_a[l]),
                                  _tile_lanes(q_norm_b[l]), _tile_lanes(k_norm_b[l])], axis=0),
        "gmat": jnp.asarray(gmat, BF16),
        "w_out_a": w_out[l][:SEC].astype(BF16),
        "w_out_b": w_out[l][SEC:].astype(BF16),
        "g_ffn": g_ffn[l].reshape(1, D_MODEL),
        "w_r_hi": w_r_hi,
        "w_r_lo": (w_r - w_r_hi.astype(F32)).astype(BF16),
        "b_r": b_r,
        "w_g": w_exp_gate[l].astype(BF16),
        "w_u": w_exp_up[l].astype(BF16),
        "w_d": w_exp_down[l].astype(BF16),
    }
    lam_vecs = jnp.stack([lambda_q1[l], lambda_k1[l], lambda_q2[l], lambda_k2[l]]).astype(F32)
    subln = subln_b[l].reshape(1, LANES)

    tm_p = 256
    cos_p, sin_p = _rope_tables(jnp.arange(seq))
    xp = x_prompt.reshape(batch * seq, D_MODEL)
    (akf, avf, bkf, bvf, ikf, iwf, aqb, akb, avb, bqb, bkb, bvb, iqb, ikb) = _project(
        xp, dict(consts, cos=cos_p, sin=sin_p), tm_p, seq)
    a_out = _dsa_prompt(iqb, iwf, aqb, ikb, akb, avb, batch, seq, 128)
    b_out = _diff_prompt(bqb, bkb, bvb, lam_vecs, subln, batch, seq, 128, lam_init)
    h_p, hn_p, gates_p = _merge(xp, a_out, b_out, consts, tm_p)
    y_p = _moe(hn_p, gates_p, h_p, consts, 1024)

    ns = nb * n_new
    tm_s = 256
    cos_s, sin_s = _rope_tables(past + (jnp.arange(tm_s) % n_new))
    xs = x_sample.reshape(ns, D_MODEL)
    (akf_s, avf_s, bkf_s, bvf_s, ikf_s, iwf_s, aqb_s, _, _, bqb_s, _, _, iqb_s, _) = _project(
        xs, dict(consts, cos=cos_s, sin=sin_s), tm_s, tm_s)
    a_out_s, b_out_s = _sample_attention(
        page_table,
        iqb_s.astype(F32).reshape(nb, n_new * 8, HEAD_DIM),
        iwf_s[:, :8].reshape(nb, n_new * 8, 1),
        aqb_s.astype(F32).reshape(nb, n_new, SEC),
        bqb_s.astype(F32).reshape(nb, n_new, SEC),
        ikf_s[:, :HEAD_DIM].reshape(nb, n_new, HEAD_DIM),
        akf_s.reshape(nb, n_new, SEC), avf_s.reshape(nb, n_new, SEC),
        bkf_s.reshape(nb, n_new, SEC), bvf_s.reshape(nb, n_new, SEC),
        lam_vecs, subln,
        cache_idx_k, cache_a_k.reshape(cache_a_k.shape[:3] + (SEC,)),
        cache_a_v.reshape(cache_a_v.shape[:3] + (SEC,)),
        cache_b_k.reshape(cache_b_k.shape[:3] + (SEC,)),
        cache_b_v.reshape(cache_b_v.shape[:3] + (SEC,)),
        lam_init)
    h_s, hn_s, gates_s = _merge(xs, a_out_s.reshape(ns, SEC), b_out_s.reshape(ns, SEC), consts, tm_s)
    y_s = _moe(hn_s, gates_s, h_s, consts, ns)

    a_heads = SEC // HEAD_DIM
    b_heads = SEC // LANES
    return (
        y_p.reshape(batch, seq, D_MODEL),
        y_s.reshape(nb, n_new, D_MODEL),
        akf.reshape(1, batch, seq, a_heads, HEAD_DIM),
        avf.reshape(1, batch, seq, a_heads, HEAD_DIM),
        ikf[:, :HEAD_DIM].reshape(1, batch, seq, HEAD_DIM),
        bkf.reshape(1, batch, seq, b_heads, 2, HEAD_DIM),
        bvf.reshape(1, batch, seq, b_heads, LANES),
        akf_s.reshape(1, nb, n_new, a_heads, HEAD_DIM),
        avf_s.reshape(1, nb, n_new, a_heads, HEAD_DIM),
        ikf_s[:, :HEAD_DIM].reshape(1, nb, n_new, HEAD_DIM),
        bkf_s.reshape(1, nb, n_new, b_heads, 2, HEAD_DIM),
        bvf_s.reshape(1, nb, n_new, b_heads, LANES),
    )
```

```python
import functools
import math

import jax
import jax.numpy as jnp
import numpy as np
from jax import lax
from jax.experimental import pallas as pl
from jax.experimental.pallas import tpu as pltpu

F32 = jnp.float32
BF16 = jnp.bfloat16
I32 = jnp.int32

D_MODEL = 1024
HEAD_DIM = 64
LANES = 128
SEC = 512
N_SEC = 7
PAGE = 128
TOPK = 256
N_GROUPS = 4
EXPERTS_PER_GROUP = 8
N_EXPERTS = N_GROUPS * EXPERTS_PER_GROUP
D_EXPERT = 256
ROPE_THETA = 10000.0
NORM_EPS = 1e-6
NEG = -1e30
INT_MIN = -2 ** 31
VMEM_LIMIT = 56 * 1024 * 1024


def _dot(a, b):
    return jnp.dot(a, b, preferred_element_type=F32)


def _dot_nt(a, b):
    return lax.dot_general(a, b, (((1,), (1,)), ((), ())), preferred_element_type=F32)


def _lane_iota(shape):
    return lax.broadcasted_iota(I32, shape, len(shape) - 1)


def _row_iota(shape):
    return lax.broadcasted_iota(I32, shape, len(shape) - 2)


def _proj_kernel(x_ref, g_ref, wm_ref, wt_ref, gains_ref, cos_ref, sin_ref, gmat_ref,
                 akf_ref, avf_ref, bkf_ref, bvf_ref, ikf_ref, iwf_ref,
                 aqb_ref, akb_ref, avb_ref, bqb_ref, bkb_ref, bvb_ref, iqb_ref, ikb_ref):
    xf = x_ref[...]
    ms = jnp.mean(xf * xf, axis=-1, keepdims=True)
    xn = (xf * lax.rsqrt(ms + NORM_EPS) * g_ref[...]).astype(BF16)
    cos = cos_ref[...]
    sin = sin_ref[...]
    tm = xf.shape[0]
    first_half = (_lane_iota((tm, LANES)) % HEAD_DIM) < (HEAD_DIM // 2)
    gmat = gmat_ref[...]

    def rope(c):
        swapped = jnp.where(first_half, pltpu.roll(c, LANES - HEAD_DIM // 2, 1),
                            pltpu.roll(c, HEAD_DIM // 2, 1))
        return c * cos + swapped * sin

    def headnorm(c, gain):
        ssq = _dot((c * c).astype(BF16), gmat)
        return c * lax.rsqrt(ssq * (1.0 / HEAD_DIM) + NORM_EPS) * gain

    plan = ((0, True, None, aqb_ref), (1, True, akf_ref, akb_ref), (None, False, avf_ref, avb_ref),
            (2, True, None, bqb_ref), (3, True, bkf_ref, bkb_ref), (None, False, bvf_ref, bvb_ref),
            (None, True, None, iqb_ref))
    for s, (gi, do_rope, f_ref, b_ref) in enumerate(plan):
        hs = _dot(xn, wm_ref[:, s * SEC:(s + 1) * SEC])
        for c in range(SEC // LANES):
            ch = hs[:, c * LANES:(c + 1) * LANES]
            if gi is not None:
                ch = headnorm(ch, gains_ref[gi:gi + 1, :])
            if do_rope:
                ch = rope(ch)
            if f_ref is not None:
                f_ref[:, c * LANES:(c + 1) * LANES] = ch
            b_ref[:, c * LANES:(c + 1) * LANES] = ch.astype(BF16)
    ht = _dot(xn, wt_ref[...])
    ik2 = rope(ht[:, :LANES])
    ikf_ref[...] = ik2
    ikb_ref[...] = ik2.astype(BF16)
    iwf_ref[...] = ht[:, LANES:] * (8 ** -0.5)


def _project(x2d, consts, tm, table_rows_per_period):
    n = x2d.shape[0]
    nper = table_rows_per_period // tm
    grid = (n // tm,)
    row = lambda i: (i, 0)
    fixed = lambda i: (0, 0)
    tab = (lambda i: (i % nper, 0)) if nper > 1 else fixed
    wide_f = jax.ShapeDtypeStruct((n, SEC), F32)
    wide_b = jax.ShapeDtypeStruct((n, SEC), BF16)
    lane_f = jax.ShapeDtypeStruct((n, LANES), F32)
    lane_b = jax.ShapeDtypeStruct((n, LANES), BF16)
    out_shape = (wide_f, wide_f, wide_f, wide_f, lane_f, lane_f,
                 wide_b, wide_b, wide_b, wide_b, wide_b, wide_b, wide_b, lane_b)
    wide_spec = pl.BlockSpec((tm, SEC), row)
    lane_spec = pl.BlockSpec((tm, LANES), row)
    out_specs = (wide_spec,) * 4 + (lane_spec,) * 2 + (wide_spec,) * 7 + (lane_spec,)
    return pl.pallas_call(
        _proj_kernel,
        grid=grid,
        in_specs=[
            pl.BlockSpec((tm, D_MODEL), row),
            pl.BlockSpec((1, D_MODEL), fixed),
            pl.BlockSpec((D_MODEL, N_SEC * SEC), fixed),
            pl.BlockSpec((D_MODEL, 2 * LANES), fixed),
            pl.BlockSpec((4, LANES), fixed),
            pl.BlockSpec((tm, LANES), tab),
            pl.BlockSpec((tm, LANES), tab),
            pl.BlockSpec((LANES, LANES), fixed),
        ],
        out_specs=out_specs,
        out_shape=out_shape,
        compiler_params=pltpu.CompilerParams(dimension_semantics=("arbitrary",),
                                             vmem_limit_bytes=VMEM_LIMIT),
        name="project",
    )(x2d, consts["g_mix"], consts["w_main"], consts["w_tail"], consts["gains"],
      consts["cos"], consts["sin"], consts["gmat"])


def _topk_mask(score, key_ref, ones_ref, k):
    r, l = score.shape
    nt = l // LANES
    idx_bits = (l - 1).bit_length()
    bits = pltpu.bitcast(score + 0.0, I32)
    key_ref[:, :l] = jnp.where(bits < 0, bits ^ jnp.int32(0x7FFFFFFF), bits)
    kf = jnp.float32(k)

    def rep(x):
        return jnp.concatenate([x] * nt, axis=1)

    def count(mask):
        return _dot(jnp.where(mask, 1.0, 0.0).astype(BF16), ones_ref[:l, :])

    t0 = jnp.where(count(key_ref[:, :l] >= 0) >= kf, jnp.int32(0), jnp.int32(INT_MIN))

    def value_step(i, t):
        cand = t | jnp.left_shift(jnp.int32(1), jnp.int32(30) - i)
        return jnp.where(count(key_ref[:, :l] >= rep(cand)) >= kf, cand, t)

    thr = lax.fori_loop(0, 31, value_step, t0)
    keys = key_ref[:, :l]
    thr_l = rep(thr)
    gt = keys > thr_l
    need = kf - count(gt)
    idx = _lane_iota((r, l))

    def index_step(i, j0):
        cand = j0 | jnp.left_shift(jnp.int32(1), jnp.int32(idx_bits - 1) - i)
        cnt = count((key_ref[:, :l] == rep(thr)) & (idx < rep(cand)))
        return jnp.where(cnt < need, cand, j0)

    j0 = lax.fori_loop(0, idx_bits, index_step, jnp.zeros((r, LANES), I32))
    return gt | ((keys == thr_l) & (idx <= rep(j0)))


def _causal_variants(body, seq, n_var):
    qi = pl.program_id(1)
    per = pl.num_programs(1) // n_var
    for v in range(n_var):
        pl.when(qi // per == v)(functools.partial(body, (v + 1) * (seq // n_var)))


def _dsa_kernel(iq_ref, iw_ref, aq_ref, ik_ref, ak_ref, av_ref, ones_ref, out_ref, key_ref, *, n_var):
    tq = iq_ref.shape[0]
    qi = pl.program_id(1)
    low = _lane_iota((tq, LANES)) < HEAD_DIM

    def body(l):
        ik2 = ik_ref[:l, :]
        score = jnp.zeros((tq, l), F32)
        for j in range(SEC // LANES):
            iqc = iq_ref[:, j * LANES:(j + 1) * LANES]
            for par in range(2):
                h = 2 * j + par
                lhs = jnp.where(low if par == 0 else ~low, iqc, jnp.zeros_like(iqc))
                s = _dot_nt(lhs, ik2)
                w = iw_ref[:, h:h + 1] * 0.125
                score = score + w * jnp.maximum(s, 0.0)
        causal = _lane_iota((tq, l)) <= qi * tq + _row_iota((tq, 1))
        score = jnp.where(causal, score, -jnp.inf)
        mask = _topk_mask(score, key_ref, ones_ref, TOPK) & causal
        for j in range(SEC // LANES):
            aqc = aq_ref[:, j * LANES:(j + 1) * LANES]
            kc = ak_ref[:l, j * LANES:(j + 1) * LANES]
            vc = av_ref[:l, j * LANES:(j + 1) * LANES]
            outs = []
            for par in range(2):
                lhs = jnp.where(low if par == 0 else ~low, aqc, jnp.zeros_like(aqc))
                logits = jnp.where(mask, _dot_nt(lhs, kc) * 0.125, NEG)
                m = jnp.max(logits, axis=1, keepdims=True)
                p = jnp.exp(logits - m)
                den = jnp.sum(p, axis=1, keepdims=True)
                outs.append(_dot(p.astype(BF16), vc) / den)
            out_ref[:, j * LANES:(j + 1) * LANES] = jnp.where(low, outs[0], outs[1])

    _causal_variants(body, ik_ref.shape[0], n_var)


def _dsa_prompt(iq, iw, aq, ik2, ak, av, ones, batch, seq, tq, n_var):
    nq = seq // tq
    qrow = lambda b, q: (b * nq + q, 0)
    krow = lambda b, q: (b, 0)
    return pl.pallas_call(
        functools.partial(_dsa_kernel, n_var=n_var),
        grid=(batch, nq),
        in_specs=[
            pl.BlockSpec((tq, SEC), qrow),
            pl.BlockSpec((tq, LANES), qrow),
            pl.BlockSpec((tq, SEC), qrow),
            pl.BlockSpec((seq, LANES), krow),
            pl.BlockSpec((seq, SEC), krow),
            pl.BlockSpec((seq, SEC), krow),
            pl.BlockSpec((seq, LANES), lambda b, q: (0, 0)),
        ],
        out_specs=pl.BlockSpec((tq, SEC), qrow),
        out_shape=jax.ShapeDtypeStruct((batch * seq, SEC), F32),
        scratch_shapes=[pltpu.VMEM((tq, seq), I32)],
        compiler_params=pltpu.CompilerParams(dimension_semantics=("arbitrary", "arbitrary"),
                                             vmem_limit_bytes=VMEM_LIMIT),
        name="dsa_prompt",
    )(iq, iw, aq, ik2, ak, av, ones)


def _lambda_value(lam_ref, lam_init):
    lq1, lk1, lq2, lk2 = (lam_ref[i:i + 1, :] for i in range(4))
    s1 = jnp.sum(lq1 * lk1, axis=1, keepdims=True)
    s2 = jnp.sum(lq2 * lk2, axis=1, keepdims=True)
    return jnp.exp(s1) - jnp.exp(s2) + lam_init


def _subln(o, gain, lam_init):
    ms = jnp.mean(o * o, axis=-1, keepdims=True)
    return o * lax.rsqrt(ms + NORM_EPS) * gain * (1.0 - lam_init)


def _diff_kernel(bq_ref, bk_ref, bv_ref, lam_ref, subln_ref, out_ref, *, lam_init, n_var):
    tq = bq_ref.shape[0]
    qi = pl.program_id(1)
    low = _lane_iota((tq, LANES)) < HEAD_DIM
    lam = _lambda_value(lam_ref, lam_init)

    def body(l):
        causal = _lane_iota((tq, l)) <= qi * tq + _row_iota((tq, 1))
        for h in range(SEC // LANES):
            qc = bq_ref[:, h * LANES:(h + 1) * LANES]
            kc = bk_ref[:l, h * LANES:(h + 1) * LANES]
            vc = bv_ref[:l, h * LANES:(h + 1) * LANES]
            probs = []
            for c in range(2):
                lhs = jnp.where(low if c == 0 else ~low, qc, jnp.zeros_like(qc))
                logits = jnp.where(causal, _dot_nt(lhs, kc) * 0.125, NEG)
                m = jnp.max(logits, axis=1, keepdims=True)
                p = jnp.exp(logits - m)
                probs.append(p / jnp.sum(p, axis=1, keepdims=True))
            a = probs[0] - lam * probs[1]
            o = _dot(a.astype(BF16), vc)
            out_ref[:, h * LANES:(h + 1) * LANES] = _subln(o, subln_ref[...], lam_init)

    _causal_variants(body, bk_ref.shape[0], n_var)


def _diff_prompt(bq, bk, bv, lam_vecs, subln, batch, seq, tq, lam_init, n_var):
    nq = seq // tq
    qrow = lambda b, q: (b * nq + q, 0)
    krow = lambda b, q: (b, 0)
    fixed = lambda b, q: (0, 0)
    return pl.pallas_call(
        functools.partial(_diff_kernel, lam_init=lam_init, n_var=n_var),
        grid=(batch, nq),
        in_specs=[
            pl.BlockSpec((tq, SEC), qrow),
            pl.BlockSpec((seq, SEC), krow),
            pl.BlockSpec((seq, SEC), krow),
            pl.BlockSpec((4, HEAD_DIM), fixed),
            pl.BlockSpec((1, LANES), fixed),
        ],
        out_specs=pl.BlockSpec((tq, SEC), qrow),
        out_shape=jax.ShapeDtypeStruct((batch * seq, SEC), F32),
        compiler_params=pltpu.CompilerParams(dimension_semantics=("arbitrary", "arbitrary"),
                                             vmem_limit_bytes=VMEM_LIMIT),
        name="diff_prompt",
    )(bq, bk, bv, lam_vecs, subln)


def _sample_kernel(pt_ref, iqr_ref, iwr_ref, aq_ref, bq_ref, ikn_ref, akn_ref, avn_ref, bkn_ref, bvn_ref,
                   lam_ref, subln_ref, ones_ref, *rest, n_pages, n_new, lam_init):
    del pt_ref
    assert n_new == 4
    pages = rest[:5 * n_pages]
    idx_pages = pages[0 * n_pages:1 * n_pages]
    ak_pages = pages[1 * n_pages:2 * n_pages]
    av_pages = pages[2 * n_pages:3 * n_pages]
    bk_pages = pages[3 * n_pages:4 * n_pages]
    bv_pages = pages[4 * n_pages:5 * n_pages]
    aout_ref, bout_ref, key_ref, new_ref = rest[5 * n_pages:]
    past = n_pages * PAGE
    ltot = past + PAGE
    rows = 8 * n_new
    lam = _lambda_value(lam_ref, lam_init)

    def padded_new(src_ref):
        width = src_ref.shape[-1]
        new_ref[:, :width] = jnp.zeros((PAGE, width), F32)
        new_ref[0:n_new, :width] = src_ref[0]
        return new_ref[:, :width].astype(BF16)

    lane = _lane_iota((8, ltot))
    newpos = lane - past
    iq = iqr_ref[0].astype(BF16)
    parts = [_dot(iq, idx_pages[p][0, 0].astype(BF16)) for p in range(n_pages)]
    parts.append(_dot_nt(iq, padded_new(ikn_ref)))
    s = jnp.concatenate(parts, axis=1)
    s = jnp.maximum(s, 0.0) * (iwr_ref[0] * 0.125)
    rowid = _row_iota((8, ltot))
    score = jnp.full((8, ltot), -jnp.inf, F32)
    causal8 = jnp.zeros((8, ltot), jnp.bool_)
    for i in range(n_new):
        si = jnp.sum(s[8 * i:8 * (i + 1), :], axis=0, keepdims=True)
        ok = (lane < past) | ((newpos >= 0) & (newpos <= i))
        score = jnp.where((rowid == i) & ok, jnp.broadcast_to(si, (8, ltot)), score)
        causal8 = causal8 | ((rowid == i) & ok)
    sel8 = _topk_mask(score, key_ref, ones_ref, TOPK) & causal8
    sel8f = jnp.where(sel8, 1.0, 0.0)

    lane_s = _lane_iota((8, SEC))
    row_s = _row_iota((8, SEC))

    def softmax_parts(qbd, k_pages, kn_ref, mask):
        lparts = [_dot(qbd, k_pages[p][0, 0].astype(BF16)) for p in range(n_pages)]
        lparts.append(_dot_nt(qbd, padded_new(kn_ref)))
        logits = jnp.where(mask, jnp.concatenate(lparts, axis=1) * 0.125, NEG)
        m = jnp.max(logits, axis=1, keepdims=True)
        p = jnp.exp(logits - m)
        return p, jnp.sum(p, axis=1, keepdims=True)

    slot_mask = (lane_s // HEAD_DIM) == row_s
    qa = aq_ref[0]
    qbd_a = jnp.concatenate(
        [jnp.where(slot_mask, jnp.broadcast_to(qa[i:i + 1, :], (8, SEC)), 0.0) for i in range(n_new)],
        axis=0).astype(BF16)
    mask_a = jnp.concatenate(
        [jnp.broadcast_to(sel8f[i:i + 1, :], (8, ltot)) for i in range(n_new)], axis=0) > 0.5
    p, den = softmax_parts(qbd_a, ak_pages, akn_ref, mask_a)
    pb = p.astype(BF16)
    oa = _dot(pb[:, past:], padded_new(avn_ref))
    for pg in range(n_pages):
        oa = oa + _dot_nt(pb[:, pg * PAGE:(pg + 1) * PAGE], av_pages[pg][0, 0].astype(BF16))
    oa = oa / den
    for i in range(n_new):
        blk = jnp.where(slot_mask, oa[8 * i:8 * (i + 1), :], 0.0)
        aout_ref[0, i:i + 1, :] = jnp.sum(blk, axis=0, keepdims=True)

    qb = bq_ref[0]
    q8 = jnp.zeros((8, SEC), F32)
    for i in range(n_new):
        q8 = jnp.where(row_s % n_new == i, jnp.broadcast_to(qb[i:i + 1, :], (8, SEC)), q8)
    qbd_b = jnp.concatenate(
        [jnp.where((lane_s // HEAD_DIM) == 2 * h + row_s // n_new, q8, 0.0) for h in range(SEC // LANES)],
        axis=0).astype(BF16)
    causal_b8 = jnp.zeros((8, ltot), jnp.bool_)
    for i in range(n_new):
        causal_b8 = causal_b8 | ((rowid % n_new == i) & ((lane < past) | ((newpos >= 0) & (newpos <= i))))
    causal_b8f = jnp.where(causal_b8, 1.0, 0.0)
    mask_b = jnp.concatenate([causal_b8f] * (SEC // LANES), axis=0) > 0.5
    p, den = softmax_parts(qbd_b, bk_pages, bkn_ref, mask_b)
    p_new = p[:, past:].astype(BF16)
    ob_new = _dot(p_new, padded_new(bvn_ref))
    row_v = _row_iota((8, LANES))
    sign = jnp.where(row_v // n_new == 0, 1.0, -lam)
    for h in range(SEC // LANES):
        o = ob_new[8 * h:8 * (h + 1), h * LANES:(h + 1) * LANES]
        for pg in range(n_pages):
            ph = p[8 * h:8 * (h + 1), pg * PAGE:(pg + 1) * PAGE].astype(BF16)
            vh = bv_pages[pg][pl.ds(0, 1), pl.ds(0, 1), pl.ds(h, PAGE, stride=SEC // LANES), :]
            vh = vh[0, 0].astype(BF16)
            o = o + _dot(ph, vh)
        y = o / den[8 * h:8 * (h + 1), :] * sign
        z = y + pltpu.roll(y, n_new, 0)
        bout_ref[0, :, h * LANES:(h + 1) * LANES] = _subln(z, subln_ref[...], lam_init)[0:n_new, :]


def _sample_attention(page_table, iq_r, iw_r, aq, bq, ik_new, ak_new, av_new, bk_new, bv_new,
                      lam_vecs, subln, ones, cache_idx, cache_ak, cache_av, cache_bk, cache_bv, lam_init):
    nb, n_pages = page_table.shape
    n_new = aq.shape[1]
    rows = 8 * n_new
    ltot = n_pages * PAGE + PAGE
    seq3 = lambda b, pt: (b, 0, 0)
    fixed = lambda b, pt: (0, 0)

    def page_spec(rows_per_page, p):
        return pl.BlockSpec((1, 1, rows_per_page, PAGE), lambda b, pt, p=p: (0, pt[b, p], 0, 0))

    in_specs = [
        pl.BlockSpec((1, rows, HEAD_DIM), seq3),
        pl.BlockSpec((1, rows, 1), seq3),
        pl.BlockSpec((1, n_new, SEC), seq3),
        pl.BlockSpec((1, n_new, SEC), seq3),
        pl.BlockSpec((1, n_new, HEAD_DIM), seq3),
        pl.BlockSpec((1, n_new, SEC), seq3),
        pl.BlockSpec((1, n_new, SEC), seq3),
        pl.BlockSpec((1, n_new, SEC), seq3),
        pl.BlockSpec((1, n_new, SEC), seq3),
        pl.BlockSpec((4, HEAD_DIM), fixed),
        pl.BlockSpec((1, LANES), fixed),
        pl.BlockSpec((ltot, LANES), fixed),
    ]
    operands = [iq_r, iw_r, aq, bq, ik_new, ak_new, av_new, bk_new, bv_new, lam_vecs, subln, ones]
    for cache in (cache_idx, cache_ak, cache_av, cache_bk, cache_bv):
        for p in range(n_pages):
            in_specs.append(page_spec(cache.shape[2], p))
            operands.append(cache)
    out_spec = pl.BlockSpec((1, n_new, SEC), seq3)
    grid_spec = pltpu.PrefetchScalarGridSpec(
        num_scalar_prefetch=1,
        grid=(nb,),
        in_specs=in_specs,
        out_specs=(out_spec, out_spec),
        scratch_shapes=[pltpu.VMEM((8, ltot), I32), pltpu.VMEM((PAGE, SEC), F32)],
    )
    return pl.pallas_call(
        functools.partial(_sample_kernel, n_pages=n_pages, n_new=n_new, lam_init=lam_init),
        grid_spec=grid_spec,
        out_shape=(jax.ShapeDtypeStruct((nb, n_new, SEC), F32),) * 2,
        compiler_params=pltpu.CompilerParams(dimension_semantics=("arbitrary",),
                                             vmem_limit_bytes=VMEM_LIMIT),
        name="sample_attention",
    )(page_table, *operands)


def _merge_kernel(x_ref, a_ref, b_ref, woa_ref, wob_ref, g_ref, wrh_ref, wrl_ref, br_ref,
                  h_ref, hn_ref, gate_ref):
    h = (x_ref[...] + _dot(a_ref[...].astype(BF16), woa_ref[...])
         + _dot(b_ref[...].astype(BF16), wob_ref[...]))
    h_ref[...] = h
    ms = jnp.mean(h * h, axis=-1, keepdims=True)
    hn = h * lax.rsqrt(ms + NORM_EPS) * g_ref[...]
    hi = hn.astype(BF16)
    lo = (hn - hi.astype(F32)).astype(BF16)
    hn_ref[...] = hi
    logits = (_dot(hi, wrh_ref[...]) + _dot(lo, wrh_ref[...]) + _dot(hi, wrl_ref[...])) + br_ref[...]
    tm = logits.shape[0]
    lane = _lane_iota((tm, LANES))
    big = jnp.int32(LANES)
    is_g = lane < N_GROUPS
    gl = jnp.where(is_g, logits, -jnp.inf)
    gmax = jnp.max(gl, axis=1, keepdims=True)
    g_sel = jnp.min(jnp.where(is_g & (gl == gmax), lane, big), axis=1, keepdims=True)
    g_gate = 1.0 / jnp.sum(jnp.where(is_g, jnp.exp(gl - gmax), 0.0), axis=1, keepdims=True)
    e_idx = lane - N_GROUPS
    in_grp = (e_idx >= g_sel * EXPERTS_PER_GROUP) & (e_idx < (g_sel + 1) * EXPERTS_PER_GROUP)
    el = jnp.where(in_grp, logits, -jnp.inf)
    e1 = jnp.max(el, axis=1, keepdims=True)
    i1 = jnp.min(jnp.where(in_grp & (el == e1), lane, big), axis=1, keepdims=True)
    el2 = jnp.where(lane == i1, -jnp.inf, el)
    e2 = jnp.max(el2, axis=1, keepdims=True)
    i2 = jnp.min(jnp.where(in_grp & (el2 == e2), lane, big), axis=1, keepdims=True)
    den = jnp.sum(jnp.where(in_grp, jnp.exp(el - e1), 0.0), axis=1, keepdims=True)
    p1 = 1.0 / den
    p2 = jnp.exp(e2 - e1) / den
    gate1 = g_gate * p1 / (p1 + p2)
    gate2 = g_gate * p2 / (p1 + p2)
    gate_ref[...] = (jnp.where(lane + N_GROUPS == i1, gate1, 0.0)
                     + jnp.where(lane + N_GROUPS == i2, gate2, 0.0))


def _merge(x2d, a_out, b_out, consts, tm):
    n = x2d.shape[0]
    row = lambda i: (i, 0)
    fixed = lambda i: (0, 0)
    return pl.pallas_call(
        _merge_kernel,
        grid=(n // tm,),
        in_specs=[
            pl.BlockSpec((tm, D_MODEL), row),
            pl.BlockSpec((tm, SEC), row),
            pl.BlockSpec((tm, SEC), row),
            pl.BlockSpec((SEC, D_MODEL), fixed),
            pl.BlockSpec((SEC, D_MODEL), fixed),
            pl.BlockSpec((1, D_MODEL), fixed),
            pl.BlockSpec((D_MODEL, LANES), fixed),
            pl.BlockSpec((D_MODEL, LANES), fixed),
            pl.BlockSpec((1, LANES), fixed),
        ],
        out_specs=(pl.BlockSpec((tm, D_MODEL), row), pl.BlockSpec((tm, D_MODEL), row),
                   pl.BlockSpec((tm, LANES), row)),
        out_shape=(jax.ShapeDtypeStruct((n, D_MODEL), F32), jax.ShapeDtypeStruct((n, D_MODEL), BF16),
                   jax.ShapeDtypeStruct((n, LANES), F32)),
        compiler_params=pltpu.CompilerParams(dimension_semantics=("arbitrary",),
                                             vmem_limit_bytes=VMEM_LIMIT),
        name="merge_router",
    )(x2d, a_out, b_out, consts["w_out_a"], consts["w_out_b"], consts["g_ffn"],
      consts["w_r_hi"], consts["w_r_lo"], consts["b_r"])


def _moe_kernel(hn_ref, gate_ref, h_ref, wg_ref, wu_ref, wd_ref, y_ref, acc_ref):
    e = pl.program_id(1)

    @pl.when(e == 0)
    def _():
        acc_ref[...] = jnp.zeros_like(acc_ref)

    hn = hn_ref[...]
    gates = gate_ref[...]
    lane = _lane_iota(gates.shape)
    ge = jnp.sum(jnp.where(lane == e, gates, 0.0), axis=1, keepdims=True)
    hg = _dot(hn, wg_ref[0])
    hu = _dot(hn, wu_ref[0])
    act = (hg * jax.nn.sigmoid(hg)) * hu * ge
    acc_ref[...] += _dot(act.astype(BF16), wd_ref[0])

    @pl.when(e == pl.num_programs(1) - 1)
    def _():
        y_ref[...] = h_ref[...] + acc_ref[...]


def _moe(hn, gates, h, consts, tm):
    n = hn.shape[0]
    row = lambda i, e: (i, 0)
    return pl.pallas_call(
        _moe_kernel,
        grid=(n // tm, N_EXPERTS),
        in_specs=[
            pl.BlockSpec((tm, D_MODEL), row),
            pl.BlockSpec((tm, LANES), row),
            pl.BlockSpec((tm, D_MODEL), row),
            pl.BlockSpec((1, D_MODEL, D_EXPERT), lambda i, e: (e, 0, 0)),
            pl.BlockSpec((1, D_MODEL, D_EXPERT), lambda i, e: (e, 0, 0)),
            pl.BlockSpec((1, D_EXPERT, D_MODEL), lambda i, e: (e, 0, 0)),
        ],
        out_specs=pl.BlockSpec((tm, D_MODEL), row),
        out_shape=jax.ShapeDtypeStruct((n, D_MODEL), F32),
        scratch_shapes=[pltpu.VMEM((tm, D_MODEL), F32)],
        compiler_params=pltpu.CompilerParams(dimension_semantics=("arbitrary", "arbitrary"),
                                             vmem_limit_bytes=VMEM_LIMIT),
        name="moe_experts",
    )(hn, gates, h, consts["w_g"], consts["w_u"], consts["w_d"])


def _rope_tables(positions):
    half = HEAD_DIM // 2
    inv = ROPE_THETA ** (-jnp.arange(half, dtype=F32) / half)
    ang = positions.astype(F32)[:, None] * inv[None, :]
    cos, sin = jnp.cos(ang), jnp.sin(ang)
    cos64 = jnp.concatenate([cos, cos], axis=1)
    sin64 = jnp.concatenate([-sin, sin], axis=1)
    return jnp.tile(cos64, (1, LANES // HEAD_DIM)), jnp.tile(sin64, (1, LANES // HEAD_DIM))


def _positions_last(cache):
    nd = cache.ndim
    t = jnp.transpose(cache, (0, 1) + tuple(range(3, nd)) + (2,))
    return t.reshape(t.shape[:2] + (-1, PAGE))


def _tile_lanes(v):
    return jnp.tile(v.astype(F32).reshape(1, -1), (1, LANES // v.shape[-1]))


def kernel(x_prompt, x_sample, cache_a_k, cache_a_v, cache_idx_k, cache_b_k, cache_b_v, page_table,
           g_mix, w_in, q_norm_a, k_norm_a, q_norm_b, k_norm_b, lambda_q1, lambda_k1, lambda_q2, lambda_k2,
           subln_b, w_out, g_ffn, w_router_group, b_router_group, w_router_expert, b_router_expert,
           w_exp_gate, w_exp_up, w_exp_down):
    depth = w_in.shape[0]
    assert depth == 1, "single-layer stack"
    batch, seq, _ = x_prompt.shape
    nb, n_new, _ = x_sample.shape
    n_pages = page_table.shape[1]
    past = n_pages * PAGE
    l = 0
    lam_init = 0.8 - 0.6 * math.exp(-0.3 * l)

    wide = N_SEC * SEC
    w = w_in[l]
    w_ik = w[:, wide:wide + HEAD_DIM]
    w_iw = w[:, wide + HEAD_DIM:]
    w_tail = jnp.concatenate(
        [w_ik, w_ik, w_iw, jnp.zeros((D_MODEL, LANES - w_iw.shape[1]), F32)], axis=1).astype(BF16)
    gmat = (np.arange(LANES)[:, None] // HEAD_DIM == np.arange(LANES)[None, :] // HEAD_DIM)
    w_r = jnp.concatenate([w_router_group[l], w_router_expert[l],
                           jnp.zeros((D_MODEL, LANES - N_GROUPS - N_EXPERTS), F32)], axis=1)
    w_r_hi = w_r.astype(BF16)
    b_r = jnp.concatenate([b_router_group[l], b_router_expert[l],
                           jnp.zeros((LANES - N_GROUPS - N_EXPERTS,), F32)]).reshape(1, LANES)
    consts = {
        "g_mix": g_mix[l].reshape(1, D_MODEL),
        "w_main": w[:, :wide].astype(BF16),
        "w_tail": w_tail,
        "gains": jnp.concatenate([_tile_lanes(q_norm_a[l]), _tile_lanes(k_norm_a[l]),
                                  _tile_lanes(q_norm_b[l]), _tile_lanes(k_norm_b[l])], axis=0),
        "gmat": jnp.asarray(gmat, BF16),
        "w_out_a": w_out[l][:SEC].astype(BF16),
        "w_out_b": w_out[l][SEC:].astype(BF16),
        "g_ffn": g_ffn[l].reshape(1, D_MODEL),
        "w_r_hi": w_r_hi,
        "w_r_lo": (w_r - w_r_hi.astype(F32)).astype(BF16),
        "b_r": b_r,
        "w_g": w_exp_gate[l].astype(BF16),
        "w_u": w_exp_up[l].astype(BF16),
        "w_d": w_exp_down[l].astype(BF16),
    }
    lam_vecs = jnp.stack([lambda_q1[l], lambda_k1[l], lambda_q2[l], lambda_k2[l]]).astype(F32)
    subln = subln_b[l].reshape(1, LANES)

    tm_p = 256
    cos_p, sin_p = _rope_tables(jnp.arange(seq))
    xp = x_prompt.reshape(batch * seq, D_MODEL)
    (akf, avf, bkf, bvf, ikf, iwf, aqb, akb, avb, bqb, bkb, bvb, iqb, ikb) = _project(
        xp, dict(consts, cos=cos_p, sin=sin_p), tm_p, seq)
    ones = jnp.ones((past + PAGE, LANES), BF16)
    a_out = _dsa_prompt(iqb, iwf, aqb, ikb, akb, avb, ones[:seq], batch, seq, 128, 4)
    b_out = _diff_prompt(bqb, bkb, bvb, lam_vecs, subln, batch, seq, 128, lam_init, 4)
    h_p, hn_p, gates_p = _merge(xp, a_out, b_out, consts, tm_p)
    y_p = _moe(hn_p, gates_p, h_p, consts, 1024)

    ns = nb * n_new
    tm_s = 256
    cos_s, sin_s = _rope_tables(past + (jnp.arange(tm_s) % n_new))
    xs = x_sample.reshape(ns, D_MODEL)
    (akf_s, avf_s, bkf_s, bvf_s, ikf_s, iwf_s, aqb_s, _, _, bqb_s, _, _, iqb_s, _) = _project(
        xs, dict(consts, cos=cos_s, sin=sin_s), tm_s, tm_s)
    a_out_s, b_out_s = _sample_attention(
        page_table,
        iqb_s.astype(F32).reshape(nb, n_new * 8, HEAD_DIM),
        iwf_s[:, :8].reshape(nb, n_new * 8, 1),
        aqb_s.astype(F32).reshape(nb, n_new, SEC),
        bqb_s.astype(F32).reshape(nb, n_new, SEC),
        ikf_s[:, :HEAD_DIM].reshape(nb, n_new, HEAD_DIM),
        akf_s.reshape(nb, n_new, SEC), avf_s.reshape(nb, n_new, SEC),
        bkf_s.reshape(nb, n_new, SEC), bvf_s.reshape(nb, n_new, SEC),
        lam_vecs, subln, ones,
        _positions_last(cache_idx_k), _positions_last(cache_a_k), _positions_last(cache_a_v),
        _positions_last(cache_b_k),
        cache_b_v.reshape(cache_b_v.shape[:2] + (PAGE * (SEC // LANES), LANES)),
        lam_init)
    h_s, hn_s, gates_s = _merge(xs, a_out_s.reshape(ns, SEC), b_out_s.reshape(ns, SEC), consts, tm_s)
    y_s = _moe(hn_s, gates_s, h_s, consts, ns)

    a_heads = SEC // HEAD_DIM
    b_heads = SEC // LANES
    return (
        y_p.reshape(batch, seq, D_MODEL),
        y_s.reshape(nb, n_new, D_MODEL),
        akf.reshape(1, batch, seq, a_heads, HEAD_DIM),
        avf.reshape(1, batch, seq, a_heads, HEAD_DIM),
        ikf[:, :HEAD_DIM].reshape(1, batch, seq, HEAD_DIM),
        bkf.reshape(1, batch, seq, b_heads, 2, HEAD_DIM),
        bvf.reshape(1, batch, seq, b_heads, LANES),
        akf_s.reshape(1, nb, n_new, a_heads, HEAD_DIM),
        avf_s.reshape(1, nb, n_new, a_heads, HEAD_DIM),
        ikf_s[:, :HEAD_DIM].reshape(1, nb, n_new, HEAD_DIM),
        bkf_s.reshape(1, nb, n_new, b_heads, 2, HEAD_DIM),
        bvf_s.reshape(1, nb, n_new, b_heads, LANES),
    )
```

```python
import functools
import math

import jax
import jax.numpy as jnp
import numpy as np
from jax import lax
from jax.experimental import pallas as pl
from jax.experimental.pallas import tpu as pltpu

F32 = jnp.float32
BF16 = jnp.bfloat16
I32 = jnp.int32

D_MODEL = 1024
HEAD_DIM = 64
LANES = 128
SEC = 512
N_SEC = 7
PAGE = 128
TOPK = 256
TOPK_ROWS = 128
COUNT_FOLD = 4
N_GROUPS = 4
EXPERTS_PER_GROUP = 8
N_EXPERTS = N_GROUPS * EXPERTS_PER_GROUP
D_EXPERT = 256
ROPE_THETA = 10000.0
NORM_EPS = 1e-6
NEG = -1e30
INT_MIN = -2 ** 31
VMEM_LIMIT = 56 * 1024 * 1024


def _dot(a, b):
    return jnp.dot(a, b, preferred_element_type=F32)


def _dot_nt(a, b):
    return lax.dot_general(a, b, (((1,), (1,)), ((), ())), preferred_element_type=F32)


def _lane_iota(shape):
    return lax.broadcasted_iota(I32, shape, len(shape) - 1)


def _row_iota(shape):
    return lax.broadcasted_iota(I32, shape, len(shape) - 2)


def _proj_kernel(x_ref, g_ref, wm_ref, wt_ref, gains_ref, cos_ref, sin_ref, gmat_ref,
                 akf_ref, avf_ref, bkf_ref, bvf_ref, ikf_ref, iwf_ref,
                 aqb_ref, akb_ref, avb_ref, bqb_ref, bkb_ref, bvb_ref, iqb_ref, ikb_ref):
    xf = x_ref[...]
    ms = jnp.mean(xf * xf, axis=-1, keepdims=True)
    xn = (xf * lax.rsqrt(ms + NORM_EPS) * g_ref[...]).astype(BF16)
    cos = cos_ref[...]
    sin = sin_ref[...]
    tm = xf.shape[0]
    first_half = (_lane_iota((tm, LANES)) % HEAD_DIM) < (HEAD_DIM // 2)
    gmat = gmat_ref[...]

    def rope(c):
        swapped = jnp.where(first_half, pltpu.roll(c, LANES - HEAD_DIM // 2, 1),
                            pltpu.roll(c, HEAD_DIM // 2, 1))
        return c * cos + swapped * sin

    def headnorm(c, gain):
        ssq = _dot((c * c).astype(BF16), gmat)
        return c * lax.rsqrt(ssq * (1.0 / HEAD_DIM) + NORM_EPS) * gain

    plan = ((0, True, None, aqb_ref), (1, True, akf_ref, akb_ref), (None, False, avf_ref, avb_ref),
            (2, True, None, bqb_ref), (3, True, bkf_ref, bkb_ref), (None, False, bvf_ref, bvb_ref),
            (None, True, None, iqb_ref))
    for s, (gi, do_rope, f_ref, b_ref) in enumerate(plan):
        hs = _dot(xn, wm_ref[:, s * SEC:(s + 1) * SEC])
        for c in range(SEC // LANES):
            ch = hs[:, c * LANES:(c + 1) * LANES]
            if gi is not None:
                ch = headnorm(ch, gains_ref[gi:gi + 1, :])
            if do_rope:
                ch = rope(ch)
            if f_ref is not None:
                f_ref[:, c * LANES:(c + 1) * LANES] = ch
            b_ref[:, c * LANES:(c + 1) * LANES] = ch.astype(BF16)
    ht = _dot(xn, wt_ref[...])
    ik2 = rope(ht[:, :LANES])
    ikf_ref[...] = ik2
    ikb_ref[...] = ik2.astype(BF16)
    iwf_ref[...] = ht[:, LANES:] * (8 ** -0.5)


def _project(x2d, consts, tm, table_rows_per_period):
    n = x2d.shape[0]
    nper = table_rows_per_period // tm
    grid = (n // tm,)
    row = lambda i: (i, 0)
    fixed = lambda i: (0, 0)
    tab = (lambda i: (i % nper, 0)) if nper > 1 else fixed
    wide_f = jax.ShapeDtypeStruct((n, SEC), F32)
    wide_b = jax.ShapeDtypeStruct((n, SEC), BF16)
    lane_f = jax.ShapeDtypeStruct((n, LANES), F32)
    lane_b = jax.ShapeDtypeStruct((n, LANES), BF16)
    out_shape = (wide_f, wide_f, wide_f, wide_f, lane_f, lane_f,
                 wide_b, wide_b, wide_b, wide_b, wide_b, wide_b, wide_b, lane_b)
    wide_spec = pl.BlockSpec((tm, SEC), row)
    lane_spec = pl.BlockSpec((tm, LANES), row)
    out_specs = (wide_spec,) * 4 + (lane_spec,) * 2 + (wide_spec,) * 7 + (lane_spec,)
    return pl.pallas_call(
        _proj_kernel,
        grid=grid,
        in_specs=[
            pl.BlockSpec((tm, D_MODEL), row),
            pl.BlockSpec((1, D_MODEL), fixed),
            pl.BlockSpec((D_MODEL, N_SEC * SEC), fixed),
            pl.BlockSpec((D_MODEL, 2 * LANES), fixed),
            pl.BlockSpec((4, LANES), fixed),
            pl.BlockSpec((tm, LANES), tab),
            pl.BlockSpec((tm, LANES), tab),
            pl.BlockSpec((LANES, LANES), fixed),
        ],
        out_specs=out_specs,
        out_shape=out_shape,
        compiler_params=pltpu.CompilerParams(dimension_semantics=("arbitrary",),
                                             vmem_limit_bytes=VMEM_LIMIT),
        name="project",
    )(x2d, consts["g_mix"], consts["w_main"], consts["w_tail"], consts["gains"],
      consts["cos"], consts["sin"], consts["gmat"])


def _topk_mask(score, key_ref, ones_ref, k, n_chain=1):
    rows, l = score.shape
    r = rows // n_chain
    nt = l // LANES
    idx_bits = (l - 1).bit_length()
    bits = pltpu.bitcast(score + 0.0, I32)
    key_ref[:, :l] = jnp.where(bits < 0, bits ^ jnp.int32(0x7FFFFFFF), bits)
    kf = jnp.float32(k)
    chains = range(n_chain)

    def keys_of(c):
        return key_ref[c * r:(c + 1) * r, :l]

    def rep(x):
        return jnp.concatenate([x] * nt, axis=1)

    n_fold = -(-nt // COUNT_FOLD)

    def count(mask):
        ind = jnp.where(mask, 1.0, 0.0)
        tiles = [ind[:, t * LANES:(t + 1) * LANES] for t in range(nt)]
        folded = [functools.reduce(lambda a, b: a + b, tiles[g::n_fold]) for g in range(n_fold)]
        return _dot(jnp.concatenate(folded, axis=1).astype(BF16), ones_ref[:n_fold * LANES, :])

    t0 = tuple(jnp.where(count(keys_of(c) >= 0) >= kf, jnp.int32(0), jnp.int32(INT_MIN)) for c in chains)

    def value_step(i, ts):
        bit = jnp.left_shift(jnp.int32(1), jnp.int32(30) - i)
        return tuple(jnp.where(count(keys_of(c) >= rep(ts[c] | bit)) >= kf, ts[c] | bit, ts[c])
                     for c in chains)

    thr = lax.fori_loop(0, 31, value_step, t0)
    idx = _lane_iota((r, l))
    for c in chains:
        keys = keys_of(c)
        thr_l = rep(thr[c])
        key_ref[c * r:(c + 1) * r, :l] = jnp.where(keys > thr_l, jnp.int32(-1),
                                                   jnp.where(keys == thr_l, idx, jnp.int32(l)))

    def index_step(i, js):
        bit = jnp.left_shift(jnp.int32(1), jnp.int32(idx_bits - 1) - i)
        return tuple(jnp.where(count(keys_of(c) < rep(js[c] | bit)) < kf, js[c] | bit, js[c])
                     for c in chains)

    j0 = lax.fori_loop(0, idx_bits, index_step, tuple(jnp.zeros((r, LANES), I32) for _ in chains))
    masks = [keys_of(c) <= rep(j0[c]) for c in chains]
    return masks[0] if n_chain == 1 else jnp.concatenate(masks, axis=0)


def _causal_variants(body, seq, n_var):
    qi = pl.program_id(1)
    per = pl.num_programs(1) // n_var
    for v in range(n_var):
        pl.when(qi // per == v)(functools.partial(body, (v + 1) * (seq // n_var)))


def _dsa_kernel(iq_ref, iw_ref, aq_ref, ik_ref, ak_ref, av_ref, ones_ref, out_ref, key_ref, *, n_var):
    tq = iq_ref.shape[0]
    qi = pl.program_id(1)
    low = _lane_iota((tq, LANES)) < HEAD_DIM

    def body(l):
        ik2 = ik_ref[:l, :]
        score = jnp.zeros((tq, l), F32)
        for j in range(SEC // LANES):
            iqc = iq_ref[:, j * LANES:(j + 1) * LANES]
            for par in range(2):
                h = 2 * j + par
                lhs = jnp.where(low if par == 0 else ~low, iqc, jnp.zeros_like(iqc))
                s = _dot_nt(lhs, ik2)
                w = iw_ref[:, h:h + 1] * 0.125
                score = score + w * jnp.maximum(s, 0.0)
        causal = _lane_iota((tq, l)) <= qi * tq + _row_iota((tq, 1))
        score = jnp.where(causal, score, -jnp.inf)
        mask = _topk_mask(score, key_ref, ones_ref, TOPK, n_chain=tq // TOPK_ROWS) & causal
        for j in range(SEC // LANES):
            aqc = aq_ref[:, j * LANES:(j + 1) * LANES]
            kc = ak_ref[:l, j * LANES:(j + 1) * LANES]
            vc = av_ref[:l, j * LANES:(j + 1) * LANES]
            outs = []
            for par in range(2):
                lhs = jnp.where(low if par == 0 else ~low, aqc, jnp.zeros_like(aqc))
                logits = jnp.where(mask, _dot_nt(lhs, kc) * 0.125, NEG)
                m = jnp.max(logits, axis=1, keepdims=True)
                p = jnp.exp(logits - m)
                den = jnp.sum(p, axis=1, keepdims=True)
                outs.append(_dot(p.astype(BF16), vc) / den)
            out_ref[:, j * LANES:(j + 1) * LANES] = jnp.where(low, outs[0], outs[1])

    _causal_variants(body, ik_ref.shape[0], n_var)


def _dsa_prompt(iq, iw, aq, ik2, ak, av, ones, batch, seq, tq, n_var):
    nq = seq // tq
    qrow = lambda b, q: (b * nq + q, 0)
    krow = lambda b, q: (b, 0)
    return pl.pallas_call(
        functools.partial(_dsa_kernel, n_var=n_var),
        grid=(batch, nq),
        in_specs=[
            pl.BlockSpec((tq, SEC), qrow),
            pl.BlockSpec((tq, LANES), qrow),
            pl.BlockSpec((tq, SEC), qrow),
            pl.BlockSpec((seq, LANES), krow),
            pl.BlockSpec((seq, SEC), krow),
            pl.BlockSpec((seq, SEC), krow),
            pl.BlockSpec((seq, LANES), lambda b, q: (0, 0)),
        ],
        out_specs=pl.BlockSpec((tq, SEC), qrow),
        out_shape=jax.ShapeDtypeStruct((batch * seq, SEC), F32),
        scratch_shapes=[pltpu.VMEM((tq, seq), I32)],
        compiler_params=pltpu.CompilerParams(dimension_semantics=("arbitrary", "arbitrary"),
                                             vmem_limit_bytes=VMEM_LIMIT),
        name="dsa_prompt",
    )(iq, iw, aq, ik2, ak, av, ones)


def _lambda_value(lam_ref, lam_init):
    lq1, lk1, lq2, lk2 = (lam_ref[i:i + 1, :] for i in range(4))
    s1 = jnp.sum(lq1 * lk1, axis=1, keepdims=True)
    s2 = jnp.sum(lq2 * lk2, axis=1, keepdims=True)
    return jnp.exp(s1) - jnp.exp(s2) + lam_init


def _subln(o, gain, lam_init):
    ms = jnp.mean(o * o, axis=-1, keepdims=True)
    return o * lax.rsqrt(ms + NORM_EPS) * gain * (1.0 - lam_init)


def _diff_kernel(bq_ref, bk_ref, bv_ref, lam_ref, subln_ref, out_ref, *, lam_init, n_var):
    tq = bq_ref.shape[0]
    qi = pl.program_id(1)
    low = _lane_iota((tq, LANES)) < HEAD_DIM
    lam = _lambda_value(lam_ref, lam_init)

    def body(l):
        causal = _lane_iota((tq, l)) <= qi * tq + _row_iota((tq, 1))
        for h in range(SEC // LANES):
            qc = bq_ref[:, h * LANES:(h + 1) * LANES]
            kc = bk_ref[:l, h * LANES:(h + 1) * LANES]
            vc = bv_ref[:l, h * LANES:(h + 1) * LANES]
            probs = []
            for c in range(2):
                lhs = jnp.where(low if c == 0 else ~low, qc, jnp.zeros_like(qc))
                logits = jnp.where(causal, _dot_nt(lhs, kc) * 0.125, NEG)
                m = jnp.max(logits, axis=1, keepdims=True)
                p = jnp.exp(logits - m)
                probs.append(p / jnp.sum(p, axis=1, keepdims=True))
            a = probs[0] - lam * probs[1]
            o = _dot(a.astype(BF16), vc)
            out_ref[:, h * LANES:(h + 1) * LANES] = _subln(o, subln_ref[...], lam_init)

    _causal_variants(body, bk_ref.shape[0], n_var)


def _diff_prompt(bq, bk, bv, lam_vecs, subln, batch, seq, tq, lam_init, n_var):
    nq = seq // tq
    qrow = lambda b, q: (b * nq + q, 0)
    krow = lambda b, q: (b, 0)
    fixed = lambda b, q: (0, 0)
    return pl.pallas_call(
        functools.partial(_diff_kernel, lam_init=lam_init, n_var=n_var),
        grid=(batch, nq),
        in_specs=[
            pl.BlockSpec((tq, SEC), qrow),
            pl.BlockSpec((seq, SEC), krow),
            pl.BlockSpec((seq, SEC), krow),
            pl.BlockSpec((4, HEAD_DIM), fixed),
            pl.BlockSpec((1, LANES), fixed),
        ],
        out_specs=pl.BlockSpec((tq, SEC), qrow),
        out_shape=jax.ShapeDtypeStruct((batch * seq, SEC), F32),
        compiler_params=pltpu.CompilerParams(dimension_semantics=("arbitrary", "arbitrary"),
                                             vmem_limit_bytes=VMEM_LIMIT),
        name="diff_prompt",
    )(bq, bk, bv, lam_vecs, subln)


def _padded_new(new_ref, src_ref, n_new):
    width = src_ref.shape[-1]
    new_ref[:, :width] = jnp.zeros((PAGE, width), F32)
    new_ref[0:n_new, :width] = src_ref[0]
    return new_ref[:, :width].astype(BF16)


def _new_key_ok(shape, past, i):
    lane = _lane_iota(shape)
    return (lane < past) | ((lane >= past) & (lane - past <= i))


def _sample_score_kernel(pt_ref, iqr_ref, iwr_ref, ikn_ref, *rest, n_pages, n_new):
    del pt_ref
    idx_pages = rest[:n_pages]
    score_ref, new_ref = rest[n_pages:]
    past = n_pages * PAGE
    ltot = past + PAGE
    iq = iqr_ref[0].astype(BF16)
    parts = [_dot(iq, idx_pages[p][0, 0].astype(BF16)) for p in range(n_pages)]
    parts.append(_dot_nt(iq, _padded_new(new_ref, ikn_ref, n_new)))
    s = jnp.concatenate(parts, axis=1)
    s = jnp.maximum(s, 0.0) * (iwr_ref[0] * 0.125)
    for i in range(n_new):
        si = jnp.sum(s[8 * i:8 * (i + 1), :], axis=0, keepdims=True)
        score_ref[0, i:i + 1, :] = jnp.where(_new_key_ok((1, ltot), past, i), si, -jnp.inf)


def _sample_scores(page_table, iq_r, iw_r, ik_new, cache_idx):
    nb, n_pages = page_table.shape
    n_new = ik_new.shape[1]
    ltot = n_pages * PAGE + PAGE
    seq3 = lambda b, pt: (b, 0, 0)
    in_specs = [pl.BlockSpec((1, 8 * n_new, HEAD_DIM), seq3), pl.BlockSpec((1, 8 * n_new, 1), seq3),
                pl.BlockSpec((1, n_new, HEAD_DIM), seq3)]
    in_specs += [pl.BlockSpec((1, 1, HEAD_DIM, PAGE), lambda b, pt, p=p: (0, pt[b, p], 0, 0))
                 for p in range(n_pages)]
    grid_spec = pltpu.PrefetchScalarGridSpec(
        num_scalar_prefetch=1, grid=(nb,), in_specs=in_specs,
        out_specs=pl.BlockSpec((1, n_new, ltot), seq3),
        scratch_shapes=[pltpu.VMEM((PAGE, HEAD_DIM), F32)])
    return pl.pallas_call(
        functools.partial(_sample_score_kernel, n_pages=n_pages, n_new=n_new),
        grid_spec=grid_spec,
        out_shape=jax.ShapeDtypeStruct((nb, n_new, ltot), F32),
        compiler_params=pltpu.CompilerParams(dimension_semantics=("arbitrary",),
                                             vmem_limit_bytes=VMEM_LIMIT),
        name="sample_scores",
    )(page_table, iq_r, iw_r, ik_new, *([cache_idx] * n_pages))


def _topk_rows_kernel(score_ref, ones_ref, sel_ref, key_ref):
    score = score_ref[...]
    mask = _topk_mask(score, key_ref, ones_ref, TOPK, n_chain=score.shape[0] // TOPK_ROWS)
    sel_ref[...] = jnp.where(mask, 1.0, 0.0)


def _topk_rows(score, ones, tr):
    n, l = score.shape
    row = lambda i: (i, 0)
    return pl.pallas_call(
        _topk_rows_kernel,
        grid=(n // tr,),
        in_specs=[pl.BlockSpec((tr, l), row), pl.BlockSpec((l, LANES), lambda i: (0, 0))],
        out_specs=pl.BlockSpec((tr, l), row),
        out_shape=jax.ShapeDtypeStruct((n, l), F32),
        scratch_shapes=[pltpu.VMEM((tr, l), I32)],
        compiler_params=pltpu.CompilerParams(dimension_semantics=("arbitrary",),
                                             vmem_limit_bytes=VMEM_LIMIT),
        name="topk_rows",
    )(score, ones)


def _sample_kernel(pt_ref, sel_ref, aq_ref, bq_ref, akn_ref, avn_ref, bkn_ref, bvn_ref,
                   lam_ref, subln_ref, *rest, n_pages, n_new, lam_init):
    del pt_ref
    assert n_new == 4
    pages = rest[:4 * n_pages]
    ak_pages = pages[0 * n_pages:1 * n_pages]
    av_pages = pages[1 * n_pages:2 * n_pages]
    bk_pages = pages[2 * n_pages:3 * n_pages]
    bv_pages = pages[3 * n_pages:4 * n_pages]
    aout_ref, bout_ref, new_ref = rest[4 * n_pages:]
    past = n_pages * PAGE
    ltot = past + PAGE
    lam = _lambda_value(lam_ref, lam_init)
    padded_new = functools.partial(_padded_new, new_ref, n_new=n_new)
    rowid = _row_iota((8, ltot))
    sel = sel_ref[0]

    lane_s = _lane_iota((8, SEC))
    row_s = _row_iota((8, SEC))

    def softmax_parts(qbd, k_pages, kn_ref, mask):
        lparts = [_dot(qbd, k_pages[p][0, 0].astype(BF16)) for p in range(n_pages)]
        lparts.append(_dot_nt(qbd, padded_new(kn_ref)))
        logits = jnp.where(mask, jnp.concatenate(lparts, axis=1) * 0.125, NEG)
        m = jnp.max(logits, axis=1, keepdims=True)
        p = jnp.exp(logits - m)
        return p, jnp.sum(p, axis=1, keepdims=True)

    slot_mask = (lane_s // HEAD_DIM) == row_s
    qa = aq_ref[0]
    qbd_a = jnp.concatenate(
        [jnp.where(slot_mask, jnp.broadcast_to(qa[i:i + 1, :], (8, SEC)), 0.0) for i in range(n_new)],
        axis=0).astype(BF16)
    mask_a = jnp.concatenate(
        [jnp.broadcast_to(jnp.where(_new_key_ok((1, ltot), past, i), sel[i:i + 1, :], 0.0), (8, ltot))
         for i in range(n_new)], axis=0) > 0.5
    p, den = softmax_parts(qbd_a, ak_pages, akn_ref, mask_a)
    pb = p.astype(BF16)
    oa = _dot(pb[:, past:], padded_new(avn_ref))
    for pg in range(n_pages):
        oa = oa + _dot_nt(pb[:, pg * PAGE:(pg + 1) * PAGE], av_pages[pg][0, 0].astype(BF16))
    oa = oa / den
    for i in range(n_new):
        blk = jnp.where(slot_mask, oa[8 * i:8 * (i + 1), :], 0.0)
        aout_ref[0, i:i + 1, :] = jnp.sum(blk, axis=0, keepdims=True)

    qb = bq_ref[0]
    q8 = jnp.zeros((8, SEC), F32)
    for i in range(n_new):
        q8 = jnp.where(row_s % n_new == i, jnp.broadcast_to(qb[i:i + 1, :], (8, SEC)), q8)
    qbd_b = jnp.concatenate(
        [jnp.where((lane_s // HEAD_DIM) == 2 * h + row_s // n_new, q8, 0.0) for h in range(SEC // LANES)],
        axis=0).astype(BF16)
    causal_b8 = jnp.zeros((8, ltot), jnp.bool_)
    for i in range(n_new):
        causal_b8 = causal_b8 | ((rowid % n_new == i) & _new_key_ok((8, ltot), past, i))
    causal_b8f = jnp.where(causal_b8, 1.0, 0.0)
    mask_b = jnp.concatenate([causal_b8f] * (SEC // LANES), axis=0) > 0.5
    p, den = softmax_parts(qbd_b, bk_pages, bkn_ref, mask_b)
    p_new = p[:, past:].astype(BF16)
    ob_new = _dot(p_new, padded_new(bvn_ref))
    row_v = _row_iota((8, LANES))
    sign = jnp.where(row_v // n_new == 0, 1.0, -lam)
    for h in range(SEC // LANES):
        o = ob_new[8 * h:8 * (h + 1), h * LANES:(h + 1) * LANES]
        for pg in range(n_pages):
            ph = p[8 * h:8 * (h + 1), pg * PAGE:(pg + 1) * PAGE].astype(BF16)
            vh = bv_pages[pg][pl.ds(0, 1), pl.ds(0, 1), pl.ds(h, PAGE, stride=SEC // LANES), :]
            vh = vh[0, 0].astype(BF16)
            o = o + _dot(ph, vh)
        y = o / den[8 * h:8 * (h + 1), :] * sign
        z = y + pltpu.roll(y, n_new, 0)
        bout_ref[0, :, h * LANES:(h + 1) * LANES] = _subln(z, subln_ref[...], lam_init)[0:n_new, :]


def _sample_attention(page_table, sel, aq, bq, ak_new, av_new, bk_new, bv_new,
                      lam_vecs, subln, cache_ak, cache_av, cache_bk, cache_bv, lam_init):
    nb, n_pages = page_table.shape
    n_new = aq.shape[1]
    ltot = n_pages * PAGE + PAGE
    seq3 = lambda b, pt: (b, 0, 0)
    fixed = lambda b, pt: (0, 0)

    def page_spec(rows_per_page, p):
        return pl.BlockSpec((1, 1, rows_per_page, PAGE), lambda b, pt, p=p: (0, pt[b, p], 0, 0))

    in_specs = [pl.BlockSpec((1, n_new, ltot), seq3)]
    in_specs += [pl.BlockSpec((1, n_new, SEC), seq3)] * 6
    in_specs += [pl.BlockSpec((4, HEAD_DIM), fixed), pl.BlockSpec((1, LANES), fixed)]
    operands = [sel, aq, bq, ak_new, av_new, bk_new, bv_new, lam_vecs, subln]
    for cache in (cache_ak, cache_av, cache_bk, cache_bv):
        for p in range(n_pages):
            in_specs.append(page_spec(cache.shape[2], p))
            operands.append(cache)
    out_spec = pl.BlockSpec((1, n_new, SEC), seq3)
    grid_spec = pltpu.PrefetchScalarGridSpec(
        num_scalar_prefetch=1,
        grid=(nb,),
        in_specs=in_specs,
        out_specs=(out_spec, out_spec),
        scratch_shapes=[pltpu.VMEM((PAGE, SEC), F32)],
    )
    return pl.pallas_call(
        functools.partial(_sample_kernel, n_pages=n_pages, n_new=n_new, lam_init=lam_init),
        grid_spec=grid_spec,
        out_shape=(jax.ShapeDtypeStruct((nb, n_new, SEC), F32),) * 2,
        compiler_params=pltpu.CompilerParams(dimension_semantics=("arbitrary",),
                                             vmem_limit_bytes=VMEM_LIMIT),
        name="sample_attention",
    )(page_table, *operands)


def _merge_kernel(x_ref, a_ref, b_ref, woa_ref, wob_ref, g_ref, wrh_ref, wrl_ref, br_ref,
                  h_ref, hn_ref, gate_ref):
    h = (x_ref[...] + _dot(a_ref[...].astype(BF16), woa_ref[...])
         + _dot(b_ref[...].astype(BF16), wob_ref[...]))
    h_ref[...] = h
    ms = jnp.mean(h * h, axis=-1, keepdims=True)
    hn = h * lax.rsqrt(ms + NORM_EPS) * g_ref[...]
    hi = hn.astype(BF16)
    lo = (hn - hi.astype(F32)).astype(BF16)
    hn_ref[...] = hi
    logits = (_dot(hi, wrh_ref[...]) + _dot(lo, wrh_ref[...]) + _dot(hi, wrl_ref[...])) + br_ref[...]
    tm = logits.shape[0]
    lane = _lane_iota((tm, LANES))
    big = jnp.int32(LANES)
    is_g = lane < N_GROUPS
    gl = jnp.where(is_g, logits, -jnp.inf)
    gmax = jnp.max(gl, axis=1, keepdims=True)
    g_sel = jnp.min(jnp.where(is_g & (gl == gmax), lane, big), axis=1, keepdims=True)
    g_gate = 1.0 / jnp.sum(jnp.where(is_g, jnp.exp(gl - gmax), 0.0), axis=1, keepdims=True)
    e_idx = lane - N_GROUPS
    in_grp = (e_idx >= g_sel * EXPERTS_PER_GROUP) & (e_idx < (g_sel + 1) * EXPERTS_PER_GROUP)
    el = jnp.where(in_grp, logits, -jnp.inf)
    e1 = jnp.max(el, axis=1, keepdims=True)
    i1 = jnp.min(jnp.where(in_grp & (el == e1), lane, big), axis=1, keepdims=True)
    el2 = jnp.where(lane == i1, -jnp.inf, el)
    e2 = jnp.max(el2, axis=1, keepdims=True)
    i2 = jnp.min(jnp.where(in_grp & (el2 == e2), lane, big), axis=1, keepdims=True)
    den = jnp.sum(jnp.where(in_grp, jnp.exp(el - e1), 0.0), axis=1, keepdims=True)
    p1 = 1.0 / den
    p2 = jnp.exp(e2 - e1) / den
    gate1 = g_gate * p1 / (p1 + p2)
    gate2 = g_gate * p2 / (p1 + p2)
    gate_ref[...] = (jnp.where(lane + N_GROUPS == i1, gate1, 0.0)
                     + jnp.where(lane + N_GROUPS == i2, gate2, 0.0))


def _merge(x2d, a_out, b_out, consts, tm):
    n = x2d.shape[0]
    row = lambda i: (i, 0)
    fixed = lambda i: (0, 0)
    return pl.pallas_call(
        _merge_kernel,
        grid=(n // tm,),
        in_specs=[
            pl.BlockSpec((tm, D_MODEL), row),
            pl.BlockSpec((tm, SEC), row),
            pl.BlockSpec((tm, SEC), row),
            pl.BlockSpec((SEC, D_MODEL), fixed),
            pl.BlockSpec((SEC, D_MODEL), fixed),
            pl.BlockSpec((1, D_MODEL), fixed),
            pl.BlockSpec((D_MODEL, LANES), fixed),
            pl.BlockSpec((D_MODEL, LANES), fixed),
            pl.BlockSpec((1, LANES), fixed),
        ],
        out_specs=(pl.BlockSpec((tm, D_MODEL), row), pl.BlockSpec((tm, D_MODEL), row),
                   pl.BlockSpec((tm, LANES), row)),
        out_shape=(jax.ShapeDtypeStruct((n, D_MODEL), F32), jax.ShapeDtypeStruct((n, D_MODEL), BF16),
                   jax.ShapeDtypeStruct((n, LANES), F32)),
        compiler_params=pltpu.CompilerParams(dimension_semantics=("arbitrary",),
                                             vmem_limit_bytes=VMEM_LIMIT),
        name="merge_router",
    )(x2d, a_out, b_out, consts["w_out_a"], consts["w_out_b"], consts["g_ffn"],
      consts["w_r_hi"], consts["w_r_lo"], consts["b_r"])


def _moe_kernel(hn_ref, gate_ref, h_ref, wg_ref, wu_ref, wd_ref, y_ref, acc_ref):
    e = pl.program_id(1)

    @pl.when(e == 0)
    def _():
        acc_ref[...] = jnp.zeros_like(acc_ref)

    hn = hn_ref[...]
    gates = gate_ref[...]
    lane = _lane_iota(gates.shape)
    ge = jnp.sum(jnp.where(lane == e, gates, 0.0), axis=1, keepdims=True)
    hg = _dot(hn, wg_ref[0])
    hu = _dot(hn, wu_ref[0])
    act = (hg * jax.nn.sigmoid(hg)) * hu * ge
    acc_ref[...] += _dot(act.astype(BF16), wd_ref[0])

    @pl.when(e == pl.num_programs(1) - 1)
    def _():
        y_ref[...] = h_ref[...] + acc_ref[...]


def _moe(hn, gates, h, consts, tm):
    n = hn.shape[0]
    row = lambda i, e: (i, 0)
    return pl.pallas_call(
        _moe_kernel,
        grid=(n // tm, N_EXPERTS),
        in_specs=[
            pl.BlockSpec((tm, D_MODEL), row),
            pl.BlockSpec((tm, LANES), row),
            pl.BlockSpec((tm, D_MODEL), row),
            pl.BlockSpec((1, D_MODEL, D_EXPERT), lambda i, e: (e, 0, 0)),
            pl.BlockSpec((1, D_MODEL, D_EXPERT), lambda i, e: (e, 0, 0)),
            pl.BlockSpec((1, D_EXPERT, D_MODEL), lambda i, e: (e, 0, 0)),
        ],
        out_specs=pl.BlockSpec((tm, D_MODEL), row),
        out_shape=jax.ShapeDtypeStruct((n, D_MODEL), F32),
        scratch_shapes=[pltpu.VMEM((tm, D_MODEL), F32)],
        compiler_params=pltpu.CompilerParams(dimension_semantics=("arbitrary", "arbitrary"),
                                             vmem_limit_bytes=VMEM_LIMIT),
        name="moe_experts",
    )(hn, gates, h, consts["w_g"], consts["w_u"], consts["w_d"])


def _rope_tables(positions):
    half = HEAD_DIM // 2
    inv = ROPE_THETA ** (-jnp.arange(half, dtype=F32) / half)
    ang = positions.astype(F32)[:, None] * inv[None, :]
    cos, sin = jnp.cos(ang), jnp.sin(ang)
    cos64 = jnp.concatenate([cos, cos], axis=1)
    sin64 = jnp.concatenate([-sin, sin], axis=1)
    return jnp.tile(cos64, (1, LANES // HEAD_DIM)), jnp.tile(sin64, (1, LANES // HEAD_DIM))


def _positions_last(cache):
    nd = cache.ndim
    t = jnp.transpose(cache, (0, 1) + tuple(range(3, nd)) + (2,))
    return t.reshape(t.shape[:2] + (-1, PAGE))


def _tile_lanes(v):
    return jnp.tile(v.astype(F32).reshape(1, -1), (1, LANES // v.shape[-1]))


def kernel(x_prompt, x_sample, cache_a_k, cache_a_v, cache_idx_k, cache_b_k, cache_b_v, page_table,
           g_mix, w_in, q_norm_a, k_norm_a, q_norm_b, k_norm_b, lambda_q1, lambda_k1, lambda_q2, lambda_k2,
           subln_b, w_out, g_ffn, w_router_group, b_router_group, w_router_expert, b_router_expert,
           w_exp_gate, w_exp_up, w_exp_down):
    depth = w_in.shape[0]
    assert depth == 1, "single-layer stack"
    batch, seq, _ = x_prompt.shape
    nb, n_new, _ = x_sample.shape
    n_pages = page_table.shape[1]
    past = n_pages * PAGE
    l = 0
    lam_init = 0.8 - 0.6 * math.exp(-0.3 * l)

    wide = N_SEC * SEC
    w = w_in[l]
    w_ik = w[:, wide:wide + HEAD_DIM]
    w_iw = w[:, wide + HEAD_DIM:]
    w_tail = jnp.concatenate(
        [w_ik, w_ik, w_iw, jnp.zeros((D_MODEL, LANES - w_iw.shape[1]), F32)], axis=1).astype(BF16)
    gmat = (np.arange(LANES)[:, None] // HEAD_DIM == np.arange(LANES)[None, :] // HEAD_DIM)
    w_r = jnp.concatenate([w_router_group[l], w_router_expert[l],
                           jnp.zeros((D_MODEL, LANES - N_GROUPS - N_EXPERTS), F32)], axis=1)
    w_r_hi = w_r.astype(BF16)
    b_r = jnp.concatenate([b_router_group[l], b_router_expert[l],
                           jnp.zeros((LANES - N_GROUPS - N_EXPERTS,), F32)]).reshape(1, LANES)
    consts = {
        "g_mix": g_mix[l].reshape(1, D_MODEL),
        "w_main": w[:, :wide].astype(BF16),
        "w_tail": w_tail,
        "gains": jnp.concatenate([_tile_lanes(q_norm_a[l]), _tile_lanes(k_norm_a[l]),
                                  _tile_lanes(q_norm_b[l]), _tile_lanes(k_norm_b[l])], axis=0),
        "gmat": jnp.asarray(gmat, BF16),
        "w_out_a": w_out[l][:SEC].astype(BF16),
        "w_out_b": w_out[l][SEC:].astype(BF16),
        "g_ffn": g_ffn[l].reshape(1, D_MODEL),
        "w_r_hi": w_r_hi,
        "w_r_lo": (w_r - w_r_hi.astype(F32)).astype(BF16),
        "b_r": b_r,
        "w_g": w_exp_gate[l].astype(BF16),
        "w_u": w_exp_up[l].astype(BF16),
        "w_d": w_exp_down[l].astype(BF16),
    }
    lam_vecs = jnp.stack([lambda_q1[l], lambda_k1[l], lambda_q2[l], lambda_k2[l]]).astype(F32)
    subln = subln_b[l].reshape(1, LANES)

    tm_p = 256
    cos_p, sin_p = _rope_tables(jnp.arange(seq))
    xp = x_prompt.reshape(batch * seq, D_MODEL)
    (akf, avf, bkf, bvf, ikf, iwf, aqb, akb, avb, bqb, bkb, bvb, iqb, ikb) = _project(
        xp, dict(consts, cos=cos_p, sin=sin_p), tm_p, seq)
    ones = jnp.ones((past + PAGE, LANES), BF16)
    a_out = _dsa_prompt(iqb, iwf, aqb, ikb, akb, avb, ones[:seq], batch, seq, 256, 4)
    b_out = _diff_prompt(bqb, bkb, bvb, lam_vecs, subln, batch, seq, 128, lam_init, 4)
    h_p, hn_p, gates_p = _merge(xp, a_out, b_out, consts, tm_p)
    y_p = _moe(hn_p, gates_p, h_p, consts, 1024)

    ns = nb * n_new
    tm_s = 256
    cos_s, sin_s = _rope_tables(past + (jnp.arange(tm_s) % n_new))
    xs = x_sample.reshape(ns, D_MODEL)
    (akf_s, avf_s, bkf_s, bvf_s, ikf_s, iwf_s, aqb_s, _, _, bqb_s, _, _, iqb_s, _) = _project(
        xs, dict(consts, cos=cos_s, sin=sin_s), tm_s, tm_s)
    score_s = _sample_scores(
        page_table,
        iqb_s.astype(F32).reshape(nb, n_new * 8, HEAD_DIM),
        iwf_s[:, :8].reshape(nb, n_new * 8, 1),
        ikf_s[:, :HEAD_DIM].reshape(nb, n_new, HEAD_DIM),
        _positions_last(cache_idx_k))
    sel_s = _topk_rows(score_s.reshape(ns, past + PAGE), ones, 256).reshape(nb, n_new, past + PAGE)
    a_out_s, b_out_s = _sample_attention(
        page_table, sel_s,
        aqb_s.astype(F32).reshape(nb, n_new, SEC),
        bqb_s.astype(F32).reshape(nb, n_new, SEC),
        akf_s.reshape(nb, n_new, SEC), avf_s.reshape(nb, n_new, SEC),
        bkf_s.reshape(nb, n_new, SEC), bvf_s.reshape(nb, n_new, SEC),
        lam_vecs, subln,
        _positions_last(cache_a_k), _positions_last(cache_a_v), _positions_last(cache_b_k),
        cache_b_v.reshape(cache_b_v.shape[:2] + (PAGE * (SEC // LANES), LANES)),
        lam_init)
    h_s, hn_s, gates_s = _merge(xs, a_out_s.reshape(ns, SEC), b_out_s.reshape(ns, SEC), consts, tm_s)
    y_s = _moe(hn_s, gates_s, h_s, consts, ns)

    a_heads = SEC // HEAD_DIM
    b_heads = SEC // LANES
    return (
        y_p.reshape(batch, seq, D_MODEL),
        y_s.reshape(nb, n_new, D_MODEL),
        akf.reshape(1, batch, seq, a_heads, HEAD_DIM),
        avf.reshape(1, batch, seq, a_heads, HEAD_DIM),
        ikf[:, :HEAD_DIM].reshape(1, batch, seq, HEAD_DIM),
        bkf.reshape(1, batch, seq, b_heads, 2, HEAD_DIM),
        bvf.reshape(1, batch, seq, b_heads, LANES),
        akf_s.reshape(1, nb, n_new, a_heads, HEAD_DIM),
        avf_s.reshape(1, nb, n_new, a_heads, HEAD_DIM),
        ikf_s[:, :HEAD_DIM].reshape(1, nb, n_new, HEAD_DIM),
        bkf_s.reshape(1, nb, n_new, b_heads, 2, HEAD_DIM),
        bvf_s.reshape(1, nb, n_new, b_heads, LANES),
    )
```

```python
import functools
import math

import jax
import jax.numpy as jnp
import numpy as np
from jax import lax
from jax.experimental import pallas as pl
from jax.experimental.pallas import tpu as pltpu

F32 = jnp.float32
BF16 = jnp.bfloat16
I32 = jnp.int32

D_MODEL = 1024
HEAD_DIM = 64
LANES = 128
SEC = 512
N_SEC = 7
PAGE = 128
TOPK = 256
TOPK_ROWS = 128
COUNT_FOLD = 4
N_GROUPS = 4
EXPERTS_PER_GROUP = 8
N_EXPERTS = N_GROUPS * EXPERTS_PER_GROUP
D_EXPERT = 256
ROPE_THETA = 10000.0
NORM_EPS = 1e-6
NEG = -1e30
INT_MIN = -2 ** 31
VMEM_LIMIT = 56 * 1024 * 1024


def _dot(a, b):
    return jnp.dot(a, b, preferred_element_type=F32)


def _dot_nt(a, b):
    return lax.dot_general(a, b, (((1,), (1,)), ((), ())), preferred_element_type=F32)


def _lane_iota(shape):
    return lax.broadcasted_iota(I32, shape, len(shape) - 1)


def _row_iota(shape):
    return lax.broadcasted_iota(I32, shape, len(shape) - 2)


def _proj_kernel(x_ref, g_ref, wm_ref, wt_ref, gains_ref, cos_ref, sin_ref, gmat_ref,
                 akf_ref, avf_ref, bkf_ref, bvf_ref, ikf_ref, iwf_ref,
                 aqb_ref, akb_ref, avb_ref, bqb_ref, bkb_ref, bvb_ref, iqb_ref, ikb_ref, *, keys_transposed):
    xf = x_ref[...]
    ms = jnp.mean(xf * xf, axis=-1, keepdims=True)
    xn = (xf * lax.rsqrt(ms + NORM_EPS) * g_ref[...]).astype(BF16)
    cos = cos_ref[...]
    sin = sin_ref[...]
    tm = xf.shape[0]
    first_half = (_lane_iota((tm, LANES)) % HEAD_DIM) < (HEAD_DIM // 2)
    gmat = gmat_ref[...]

    def rope(c):
        swapped = jnp.where(first_half, pltpu.roll(c, LANES - HEAD_DIM // 2, 1),
                            pltpu.roll(c, HEAD_DIM // 2, 1))
        return c * cos + swapped * sin

    def headnorm(c, gain):
        ssq = _dot((c * c).astype(BF16), gmat)
        return c * lax.rsqrt(ssq * (1.0 / HEAD_DIM) + NORM_EPS) * gain

    def put(ref, c, val, transposed):
        if transposed:
            ref[0, c * LANES:(c + 1) * LANES, :] = val
        else:
            ref[:, c * LANES:(c + 1) * LANES] = val

    kt = keys_transposed
    plan = ((0, True, None, False, aqb_ref, False), (1, True, akf_ref, kt, akb_ref, kt),
            (None, False, avf_ref, kt, avb_ref, False),
            (2, True, None, False, bqb_ref, False), (3, True, bkf_ref, kt, bkb_ref, kt),
            (None, False, bvf_ref, False, bvb_ref, False),
            (None, True, None, False, iqb_ref, False))
    for s, (gi, do_rope, f_ref, f_t, b_ref, b_t) in enumerate(plan):
        hs = _dot(xn, wm_ref[:, s * SEC:(s + 1) * SEC])
        for c in range(SEC // LANES):
            ch = hs[:, c * LANES:(c + 1) * LANES]
            if gi is not None:
                ch = headnorm(ch, gains_ref[gi:gi + 1, :])
            if do_rope:
                ch = rope(ch)
            cht = ch.T if (f_t or b_t) else None
            if f_ref is bvf_ref and kt:
                f_ref[pl.ds(c, tm, stride=SEC // LANES), :] = ch
            elif f_ref is not None:
                put(f_ref, c, cht if f_t else ch, f_t)
            put(b_ref, c, (cht if b_t else ch).astype(BF16), b_t)
    ht = _dot(xn, wt_ref[...])
    ik2 = rope(ht[:, :LANES])
    if kt:
        ik2 = ik2.T
    put(ikf_ref, 0, ik2, kt)
    put(ikb_ref, 0, ik2.astype(BF16), kt)
    iwf_ref[...] = ht[:, LANES:] * (8 ** -0.5)


def _project(x2d, consts, tm, table_rows_per_period, keys_transposed):
    n = x2d.shape[0]
    nper = table_rows_per_period // tm
    grid = (n // tm,)
    row = lambda i: (i, 0)
    fixed = lambda i: (0, 0)
    tab = (lambda i: (i % nper, 0)) if nper > 1 else fixed

    def rows_out(width, dtype):
        return jax.ShapeDtypeStruct((n, width), dtype), pl.BlockSpec((tm, width), row)

    def cols_out(width, dtype):
        if not keys_transposed:
            return rows_out(width, dtype)
        return (jax.ShapeDtypeStruct((n // table_rows_per_period, width, table_rows_per_period), dtype),
                pl.BlockSpec((1, width, tm), lambda i: (i // nper, 0, i % nper)))

    if keys_transposed:
        bv_out = (jax.ShapeDtypeStruct((n * (SEC // LANES), LANES), F32),
                  pl.BlockSpec((tm * (SEC // LANES), LANES), row))
    else:
        bv_out = rows_out(SEC, F32)
    outs = (cols_out(SEC, F32), cols_out(SEC, F32), cols_out(SEC, F32), bv_out,
            cols_out(LANES, F32), rows_out(LANES, F32),
            rows_out(SEC, BF16), cols_out(SEC, BF16), rows_out(SEC, BF16),
            rows_out(SEC, BF16), cols_out(SEC, BF16), rows_out(SEC, BF16),
            rows_out(SEC, BF16), cols_out(LANES, BF16))
    out_shape = tuple(o[0] for o in outs)
    out_specs = tuple(o[1] for o in outs)
    return pl.pallas_call(
        functools.partial(_proj_kernel, keys_transposed=keys_transposed),
        grid=grid,
        in_specs=[
            pl.BlockSpec((tm, D_MODEL), row),
            pl.BlockSpec((1, D_MODEL), fixed),
            pl.BlockSpec((D_MODEL, N_SEC * SEC), fixed),
            pl.BlockSpec((D_MODEL, 2 * LANES), fixed),
            pl.BlockSpec((4, LANES), fixed),
            pl.BlockSpec((tm, LANES), tab),
            pl.BlockSpec((tm, LANES), tab),
            pl.BlockSpec((LANES, LANES), fixed),
        ],
        out_specs=out_specs,
        out_shape=out_shape,
        compiler_params=pltpu.CompilerParams(dimension_semantics=("arbitrary",),
                                             vmem_limit_bytes=VMEM_LIMIT),
        name="project",
    )(x2d, consts["g_mix"], consts["w_main"], consts["w_tail"], consts["gains"],
      consts["cos"], consts["sin"], consts["gmat"])


def _topk_mask(score, key_ref, ones_ref, k, n_chain=1):
    rows, l = score.shape
    r = rows // n_chain
    nt = l // LANES
    idx_bits = (l - 1).bit_length()
    bits = pltpu.bitcast(score + 0.0, I32)
    key_ref[:, :l] = jnp.where(bits < 0, bits ^ jnp.int32(0x7FFFFFFF), bits)
    kf = jnp.float32(k)
    chains = range(n_chain)

    def keys_of(c):
        return key_ref[c * r:(c + 1) * r, :l]

    def rep(x):
        return jnp.concatenate([x] * nt, axis=1)

    n_fold = -(-nt // COUNT_FOLD)

    def count(mask):
        ind = jnp.where(mask, 1.0, 0.0)
        tiles = [ind[:, t * LANES:(t + 1) * LANES] for t in range(nt)]
        folded = [functools.reduce(lambda a, b: a + b, tiles[g::n_fold]) for g in range(n_fold)]
        return _dot(jnp.concatenate(folded, axis=1).astype(BF16), ones_ref[:n_fold * LANES, :])

    t0 = tuple(jnp.where(count(keys_of(c) >= 0) >= kf, jnp.int32(0), jnp.int32(INT_MIN)) for c in chains)

    def value_step(i, ts):
        bit = jnp.left_shift(jnp.int32(1), jnp.int32(30) - i)
        return tuple(jnp.where(count(keys_of(c) >= rep(ts[c] | bit)) >= kf, ts[c] | bit, ts[c])
                     for c in chains)

    thr = lax.fori_loop(0, 31, value_step, t0)
    idx = _lane_iota((r, l))
    for c in chains:
        keys = keys_of(c)
        thr_l = rep(thr[c])
        key_ref[c * r:(c + 1) * r, :l] = jnp.where(keys > thr_l, jnp.int32(-1),
                                                   jnp.where(keys == thr_l, idx, jnp.int32(l)))

    def index_step(i, js):
        bit = jnp.left_shift(jnp.int32(1), jnp.int32(idx_bits - 1) - i)
        return tuple(jnp.where(count(keys_of(c) < rep(js[c] | bit)) < kf, js[c] | bit, js[c])
                     for c in chains)

    j0 = lax.fori_loop(0, idx_bits, index_step, tuple(jnp.zeros((r, LANES), I32) for _ in chains))
    masks = [keys_of(c) <= rep(j0[c]) for c in chains]
    return masks[0] if n_chain == 1 else jnp.concatenate(masks, axis=0)


def _causal_variants(body, seq, n_var):
    qi = pl.program_id(1)
    per = pl.num_programs(1) // n_var
    for v in range(n_var):
        pl.when(qi // per == v)(functools.partial(body, (v + 1) * (seq // n_var)))


def _dsa_kernel(iq_ref, iw_ref, aq_ref, ik_ref, ak_ref, av_ref, ones_ref, out_ref, key_ref, *, n_var):
    tq = iq_ref.shape[0]
    qi = pl.program_id(1)
    low = _lane_iota((tq, LANES)) < HEAD_DIM

    def body(l):
        ik2 = ik_ref[0, :, :l]
        score = jnp.zeros((tq, l), F32)
        for j in range(SEC // LANES):
            iqc = iq_ref[:, j * LANES:(j + 1) * LANES]
            for par in range(2):
                h = 2 * j + par
                lhs = jnp.where(low if par == 0 else ~low, iqc, jnp.zeros_like(iqc))
                s = _dot(lhs, ik2)
                w = iw_ref[:, h:h + 1] * 0.125
                score = score + w * jnp.maximum(s, 0.0)
        causal = _lane_iota((tq, l)) <= qi * tq + _row_iota((tq, 1))
        score = jnp.where(causal, score, -jnp.inf)
        mask = _topk_mask(score, key_ref, ones_ref, TOPK, n_chain=tq // TOPK_ROWS) & causal
        for j in range(SEC // LANES):
            aqc = aq_ref[:, j * LANES:(j + 1) * LANES]
            kc = ak_ref[0, j * LANES:(j + 1) * LANES, :l]
            vc = av_ref[:l, j * LANES:(j + 1) * LANES]
            outs = []
            for par in range(2):
                lhs = jnp.where(low if par == 0 else ~low, aqc, jnp.zeros_like(aqc))
                logits = jnp.where(mask, _dot(lhs, kc) * 0.125, NEG)
                m = jnp.max(logits, axis=1, keepdims=True)
                p = jnp.exp(logits - m)
                den = jnp.sum(p, axis=1, keepdims=True)
                outs.append(_dot(p.astype(BF16), vc) / den)
            out_ref[:, j * LANES:(j + 1) * LANES] = jnp.where(low, outs[0], outs[1])

    _causal_variants(body, av_ref.shape[0], n_var)


def _dsa_prompt(iq, iw, aq, ik2t, akt, av, ones, batch, seq, tq, n_var):
    nq = seq // tq
    qrow = lambda b, q: (b * nq + q, 0)
    krow = lambda b, q: (b, 0)
    kcol = lambda b, q: (b, 0, 0)
    return pl.pallas_call(
        functools.partial(_dsa_kernel, n_var=n_var),
        grid=(batch, nq),
        in_specs=[
            pl.BlockSpec((tq, SEC), qrow),
            pl.BlockSpec((tq, LANES), qrow),
            pl.BlockSpec((tq, SEC), qrow),
            pl.BlockSpec((1, LANES, seq), kcol),
            pl.BlockSpec((1, SEC, seq), kcol),
            pl.BlockSpec((seq, SEC), krow),
            pl.BlockSpec((seq, LANES), lambda b, q: (0, 0)),
        ],
        out_specs=pl.BlockSpec((tq, SEC), qrow),
        out_shape=jax.ShapeDtypeStruct((batch * seq, SEC), F32),
        scratch_shapes=[pltpu.VMEM((tq, seq), I32)],
        compiler_params=pltpu.CompilerParams(dimension_semantics=("arbitrary", "arbitrary"),
                                             vmem_limit_bytes=VMEM_LIMIT),
        name="dsa_prompt",
    )(iq, iw, aq, ik2t, akt, av, ones)


def _lambda_value(lam_ref, lam_init):
    lq1, lk1, lq2, lk2 = (lam_ref[i:i + 1, :] for i in range(4))
    s1 = jnp.sum(lq1 * lk1, axis=1, keepdims=True)
    s2 = jnp.sum(lq2 * lk2, axis=1, keepdims=True)
    return jnp.exp(s1) - jnp.exp(s2) + lam_init


def _subln(o, gain, lam_init):
    ms = jnp.mean(o * o, axis=-1, keepdims=True)
    return o * lax.rsqrt(ms + NORM_EPS) * gain * (1.0 - lam_init)


def _diff_kernel(bq_ref, bk_ref, bv_ref, lam_ref, subln_ref, out_ref, *, lam_init, n_var):
    tq = bq_ref.shape[0]
    qi = pl.program_id(1)
    low = _lane_iota((tq, LANES)) < HEAD_DIM
    lam = _lambda_value(lam_ref, lam_init)

    def body(l):
        causal = _lane_iota((tq, l)) <= qi * tq + _row_iota((tq, 1))
        for h in range(SEC // LANES):
            qc = bq_ref[:, h * LANES:(h + 1) * LANES]
            kc = bk_ref[0, h * LANES:(h + 1) * LANES, :l]
            vc = bv_ref[:l, h * LANES:(h + 1) * LANES]
            probs = []
            for c in range(2):
                lhs = jnp.where(low if c == 0 else ~low, qc, jnp.zeros_like(qc))
                logits = jnp.where(causal, _dot(lhs, kc) * 0.125, NEG)
                m = jnp.max(logits, axis=1, keepdims=True)
                p = jnp.exp(logits - m)
                probs.append(p / jnp.sum(p, axis=1, keepdims=True))
            a = probs[0] - lam * probs[1]
            o = _dot(a.astype(BF16), vc)
            out_ref[:, h * LANES:(h + 1) * LANES] = _subln(o, subln_ref[...], lam_init)

    _causal_variants(body, bv_ref.shape[0], n_var)


def _diff_prompt(bq, bkt, bv, lam_vecs, subln, batch, seq, tq, lam_init, n_var):
    nq = seq // tq
    qrow = lambda b, q: (b * nq + q, 0)
    krow = lambda b, q: (b, 0)
    fixed = lambda b, q: (0, 0)
    return pl.pallas_call(
        functools.partial(_diff_kernel, lam_init=lam_init, n_var=n_var),
        grid=(batch, nq),
        in_specs=[
            pl.BlockSpec((tq, SEC), qrow),
            pl.BlockSpec((1, SEC, seq), lambda b, q: (b, 0, 0)),
            pl.BlockSpec((seq, SEC), krow),
            pl.BlockSpec((4, HEAD_DIM), fixed),
            pl.BlockSpec((1, LANES), fixed),
        ],
        out_specs=pl.BlockSpec((tq, SEC), qrow),
        out_shape=jax.ShapeDtypeStruct((batch * seq, SEC), F32),
        compiler_params=pltpu.CompilerParams(dimension_semantics=("arbitrary", "arbitrary"),
                                             vmem_limit_bytes=VMEM_LIMIT),
        name="diff_prompt",
    )(bq, bkt, bv, lam_vecs, subln)


def _padded_new(new_ref, src_ref, n_new):
    width = src_ref.shape[-1]
    new_ref[:, :width] = jnp.zeros((PAGE, width), F32)
    new_ref[0:n_new, :width] = src_ref[0]
    return new_ref[:, :width].astype(BF16)


def _new_key_ok(shape, past, i):
    lane = _lane_iota(shape)
    return (lane < past) | ((lane >= past) & (lane - past <= i))


def _sample_score_kernel(pt_ref, iqr_ref, iwr_ref, ikn_ref, *rest, n_pages, n_new):
    del pt_ref
    idx_pages = rest[:n_pages]
    score_ref, new_ref = rest[n_pages:]
    past = n_pages * PAGE
    ltot = past + PAGE
    iq = iqr_ref[0].astype(BF16)
    parts = [_dot(iq, idx_pages[p][0, 0].astype(BF16)) for p in range(n_pages)]
    parts.append(_dot_nt(iq, _padded_new(new_ref, ikn_ref, n_new)))
    s = jnp.concatenate(parts, axis=1)
    s = jnp.maximum(s, 0.0) * (iwr_ref[0] * 0.125)
    for i in range(n_new):
        si = jnp.sum(s[8 * i:8 * (i + 1), :], axis=0, keepdims=True)
        score_ref[0, i:i + 1, :] = jnp.where(_new_key_ok((1, ltot), past, i), si, -jnp.inf)


def _sample_scores(page_table, iq_r, iw_r, ik_new, cache_idx):
    nb, n_pages = page_table.shape
    n_new = ik_new.shape[1]
    ltot = n_pages * PAGE + PAGE
    seq3 = lambda b, pt: (b, 0, 0)
    in_specs = [pl.BlockSpec((1, 8 * n_new, HEAD_DIM), seq3), pl.BlockSpec((1, 8 * n_new, 1), seq3),
                pl.BlockSpec((1, n_new, HEAD_DIM), seq3)]
    in_specs += [pl.BlockSpec((1, 1, HEAD_DIM, PAGE), lambda b, pt, p=p: (0, pt[b, p], 0, 0))
                 for p in range(n_pages)]
    grid_spec = pltpu.PrefetchScalarGridSpec(
        num_scalar_prefetch=1, grid=(nb,), in_specs=in_specs,
        out_specs=pl.BlockSpec((1, n_new, ltot), seq3),
        scratch_shapes=[pltpu.VMEM((PAGE, HEAD_DIM), F32)])
    return pl.pallas_call(
        functools.partial(_sample_score_kernel, n_pages=n_pages, n_new=n_new),
        grid_spec=grid_spec,
        out_shape=jax.ShapeDtypeStruct((nb, n_new, ltot), F32),
        compiler_params=pltpu.CompilerParams(dimension_semantics=("arbitrary",),
                                             vmem_limit_bytes=VMEM_LIMIT),
        name="sample_scores",
    )(page_table, iq_r, iw_r, ik_new, *([cache_idx] * n_pages))


def _topk_rows_kernel(score_ref, ones_ref, sel_ref, key_ref):
    score = score_ref[...]
    mask = _topk_mask(score, key_ref, ones_ref, TOPK, n_chain=score.shape[0] // TOPK_ROWS)
    sel_ref[...] = jnp.where(mask, 1.0, 0.0)


def _topk_rows(score, ones, tr):
    n, l = score.shape
    row = lambda i: (i, 0)
    return pl.pallas_call(
        _topk_rows_kernel,
        grid=(n // tr,),
        in_specs=[pl.BlockSpec((tr, l), row), pl.BlockSpec((l, LANES), lambda i: (0, 0))],
        out_specs=pl.BlockSpec((tr, l), row),
        out_shape=jax.ShapeDtypeStruct((n, l), F32),
        scratch_shapes=[pltpu.VMEM((tr, l), I32)],
        compiler_params=pltpu.CompilerParams(dimension_semantics=("arbitrary",),
                                             vmem_limit_bytes=VMEM_LIMIT),
        name="topk_rows",
    )(score, ones)


def _sample_kernel(pt_ref, sel_ref, aq_ref, bq_ref, akn_ref, avn_ref, bkn_ref, bvn_ref,
                   lam_ref, subln_ref, *rest, n_pages, n_new, lam_init):
    del pt_ref
    assert n_new == 4
    pages = rest[:4 * n_pages]
    ak_pages = pages[0 * n_pages:1 * n_pages]
    av_pages = pages[1 * n_pages:2 * n_pages]
    bk_pages = pages[2 * n_pages:3 * n_pages]
    bv_pages = pages[3 * n_pages:4 * n_pages]
    aout_ref, bout_ref, new_ref = rest[4 * n_pages:]
    past = n_pages * PAGE
    ltot = past + PAGE
    lam = _lambda_value(lam_ref, lam_init)
    padded_new = functools.partial(_padded_new, new_ref, n_new=n_new)
    rowid = _row_iota((8, ltot))
    sel = sel_ref[0]

    lane_s = _lane_iota((8, SEC))
    row_s = _row_iota((8, SEC))

    def softmax_parts(qbd, k_pages, kn_ref, mask):
        lparts = [_dot(qbd, k_pages[p][0, 0].astype(BF16)) for p in range(n_pages)]
        lparts.append(_dot_nt(qbd, padded_new(kn_ref)))
        logits = jnp.where(mask, jnp.concatenate(lparts, axis=1) * 0.125, NEG)
        m = jnp.max(logits, axis=1, keepdims=True)
        p = jnp.exp(logits - m)
        return p, jnp.sum(p, axis=1, keepdims=True)

    slot_mask = (lane_s // HEAD_DIM) == row_s
    qa = aq_ref[0]
    qbd_a = jnp.concatenate(
        [jnp.where(slot_mask, jnp.broadcast_to(qa[i:i + 1, :], (8, SEC)), 0.0) for i in range(n_new)],
        axis=0).astype(BF16)
    mask_a = jnp.concatenate(
        [jnp.broadcast_to(jnp.where(_new_key_ok((1, ltot), past, i), sel[i:i + 1, :], 0.0), (8, ltot))
         for i in range(n_new)], axis=0) > 0.5
    p, den = softmax_parts(qbd_a, ak_pages, akn_ref, mask_a)
    pb = p.astype(BF16)
    oa = _dot(pb[:, past:], padded_new(avn_ref))
    for pg in range(n_pages):
        oa = oa + _dot_nt(pb[:, pg * PAGE:(pg + 1) * PAGE], av_pages[pg][0, 0].astype(BF16))
    oa = oa / den
    for i in range(n_new):
        blk = jnp.where(slot_mask, oa[8 * i:8 * (i + 1), :], 0.0)
        aout_ref[0, i:i + 1, :] = jnp.sum(blk, axis=0, keepdims=True)

    qb = bq_ref[0]
    q8 = jnp.zeros((8, SEC), F32)
    for i in range(n_new):
        q8 = jnp.where(row_s % n_new == i, jnp.broadcast_to(qb[i:i + 1, :], (8, SEC)), q8)
    qbd_b = jnp.concatenate(
        [jnp.where((lane_s // HEAD_DIM) == 2 * h + row_s // n_new, q8, 0.0) for h in range(SEC // LANES)],
        axis=0).astype(BF16)
    causal_b8 = jnp.zeros((8, ltot), jnp.bool_)
    for i in range(n_new):
        causal_b8 = causal_b8 | ((rowid % n_new == i) & _new_key_ok((8, ltot), past, i))
    causal_b8f = jnp.where(causal_b8, 1.0, 0.0)
    mask_b = jnp.concatenate([causal_b8f] * (SEC // LANES), axis=0) > 0.5
    p, den = softmax_parts(qbd_b, bk_pages, bkn_ref, mask_b)
    p_new = p[:, past:].astype(BF16)
    ob_new = _dot(p_new, padded_new(bvn_ref))
    row_v = _row_iota((8, LANES))
    sign = jnp.where(row_v // n_new == 0, 1.0, -lam)
    for h in range(SEC // LANES):
        o = ob_new[8 * h:8 * (h + 1), h * LANES:(h + 1) * LANES]
        for pg in range(n_pages):
            ph = p[8 * h:8 * (h + 1), pg * PAGE:(pg + 1) * PAGE].astype(BF16)
            vh = bv_pages[pg][pl.ds(0, 1), pl.ds(0, 1), pl.ds(h, PAGE, stride=SEC // LANES), :]
            vh = vh[0, 0].astype(BF16)
            o = o + _dot(ph, vh)
        y = o / den[8 * h:8 * (h + 1), :] * sign
        z = y + pltpu.roll(y, n_new, 0)
        bout_ref[0, :, h * LANES:(h + 1) * LANES] = _subln(z, subln_ref[...], lam_init)[0:n_new, :]


def _sample_attention(page_table, sel, aq, bq, ak_new, av_new, bk_new, bv_new,
                      lam_vecs, subln, cache_ak, cache_av, cache_bk, cache_bv, lam_init):
    nb, n_pages = page_table.shape
    n_new = aq.shape[1]
    ltot = n_pages * PAGE + PAGE
    seq3 = lambda b, pt: (b, 0, 0)
    fixed = lambda b, pt: (0, 0)

    def page_spec(rows_per_page, p):
        return pl.BlockSpec((1, 1, rows_per_page, PAGE), lambda b, pt, p=p: (0, pt[b, p], 0, 0))

    in_specs = [pl.BlockSpec((1, n_new, ltot), seq3)]
    in_specs += [pl.BlockSpec((1, n_new, SEC), seq3)] * 6
    in_specs += [pl.BlockSpec((4, HEAD_DIM), fixed), pl.BlockSpec((1, LANES), fixed)]
    operands = [sel, aq, bq, ak_new, av_new, bk_new, bv_new, lam_vecs, subln]
    for cache in (cache_ak, cache_av, cache_bk, cache_bv):
        for p in range(n_pages):
            in_specs.append(page_spec(cache.shape[2], p))
            operands.append(cache)
    out_spec = pl.BlockSpec((1, n_new, SEC), seq3)
    grid_spec = pltpu.PrefetchScalarGridSpec(
        num_scalar_prefetch=1,
        grid=(nb,),
        in_specs=in_specs,
        out_specs=(out_spec, out_spec),
        scratch_shapes=[pltpu.VMEM((PAGE, SEC), F32)],
    )
    return pl.pallas_call(
        functools.partial(_sample_kernel, n_pages=n_pages, n_new=n_new, lam_init=lam_init),
        grid_spec=grid_spec,
        out_shape=(jax.ShapeDtypeStruct((nb, n_new, SEC), F32),) * 2,
        compiler_params=pltpu.CompilerParams(dimension_semantics=("arbitrary",),
                                             vmem_limit_bytes=VMEM_LIMIT),
        name="sample_attention",
    )(page_table, *operands)


def _merge_kernel(x_ref, a_ref, b_ref, woa_ref, wob_ref, g_ref, wrh_ref, wrl_ref, br_ref,
                  h_ref, hn_ref, gate_ref):
    h = (x_ref[...] + _dot(a_ref[...].astype(BF16), woa_ref[...])
         + _dot(b_ref[...].astype(BF16), wob_ref[...]))
    h_ref[...] = h
    ms = jnp.mean(h * h, axis=-1, keepdims=True)
    hn = h * lax.rsqrt(ms + NORM_EPS) * g_ref[...]
    hi = hn.astype(BF16)
    lo = (hn - hi.astype(F32)).astype(BF16)
    hn_ref[...] = hi
    logits = (_dot(hi, wrh_ref[...]) + _dot(lo, wrh_ref[...]) + _dot(hi, wrl_ref[...])) + br_ref[...]
    tm = logits.shape[0]
    lane = _lane_iota((tm, LANES))
    big = jnp.int32(LANES)
    is_g = lane < N_GROUPS
    gl = jnp.where(is_g, logits, -jnp.inf)
    gmax = jnp.max(gl, axis=1, keepdims=True)
    g_sel = jnp.min(jnp.where(is_g & (gl == gmax), lane, big), axis=1, keepdims=True)
    g_gate = 1.0 / jnp.sum(jnp.where(is_g, jnp.exp(gl - gmax), 0.0), axis=1, keepdims=True)
    e_idx = lane - N_GROUPS
    in_grp = (e_idx >= g_sel * EXPERTS_PER_GROUP) & (e_idx < (g_sel + 1) * EXPERTS_PER_GROUP)
    el = jnp.where(in_grp, logits, -jnp.inf)
    e1 = jnp.max(el, axis=1, keepdims=True)
    i1 = jnp.min(jnp.where(in_grp & (el == e1), lane, big), axis=1, keepdims=True)
    el2 = jnp.where(lane == i1, -jnp.inf, el)
    e2 = jnp.max(el2, axis=1, keepdims=True)
    i2 = jnp.min(jnp.where(in_grp & (el2 == e2), lane, big), axis=1, keepdims=True)
    den = jnp.sum(jnp.where(in_grp, jnp.exp(el - e1), 0.0), axis=1, keepdims=True)
    p1 = 1.0 / den
    p2 = jnp.exp(e2 - e1) / den
    gate1 = g_gate * p1 / (p1 + p2)
    gate2 = g_gate * p2 / (p1 + p2)
    gate_ref[...] = (jnp.where(lane + N_GROUPS == i1, gate1, 0.0)
                     + jnp.where(lane + N_GROUPS == i2, gate2, 0.0))


def _merge(x2d, a_out, b_out, consts, tm):
    n = x2d.shape[0]
    row = lambda i: (i, 0)
    fixed = lambda i: (0, 0)
    return pl.pallas_call(
        _merge_kernel,
        grid=(n // tm,),
        in_specs=[
            pl.BlockSpec((tm, D_MODEL), row),
            pl.BlockSpec((tm, SEC), row),
            pl.BlockSpec((tm, SEC), row),
            pl.BlockSpec((SEC, D_MODEL), fixed),
            pl.BlockSpec((SEC, D_MODEL), fixed),
            pl.BlockSpec((1, D_MODEL), fixed),
            pl.BlockSpec((D_MODEL, LANES), fixed),
            pl.BlockSpec((D_MODEL, LANES), fixed),
            pl.BlockSpec((1, LANES), fixed),
        ],
        out_specs=(pl.BlockSpec((tm, D_MODEL), row), pl.BlockSpec((tm, D_MODEL), row),
                   pl.BlockSpec((tm, LANES), row)),
        out_shape=(jax.ShapeDtypeStruct((n, D_MODEL), F32), jax.ShapeDtypeStruct((n, D_MODEL), BF16),
                   jax.ShapeDtypeStruct((n, LANES), F32)),
        compiler_params=pltpu.CompilerParams(dimension_semantics=("arbitrary",),
                                             vmem_limit_bytes=VMEM_LIMIT),
        name="merge_router",
    )(x2d, a_out, b_out, consts["w_out_a"], consts["w_out_b"], consts["g_ffn"],
      consts["w_r_hi"], consts["w_r_lo"], consts["b_r"])


def _moe_kernel(hn_ref, gate_ref, h_ref, wg_ref, wu_ref, wd_ref, y_ref, acc_ref):
    e = pl.program_id(1)

    @pl.when(e == 0)
    def _():
        acc_ref[...] = jnp.zeros_like(acc_ref)

    hn = hn_ref[...]
    gates = gate_ref[...]
    lane = _lane_iota(gates.shape)
    ge = jnp.sum(jnp.where(lane == e, gates, 0.0), axis=1, keepdims=True)
    hg = _dot(hn, wg_ref[0])
    hu = _dot(hn, wu_ref[0])
    act = (hg * jax.nn.sigmoid(hg)) * hu * ge
    acc_ref[...] += _dot(act.astype(BF16), wd_ref[0])

    @pl.when(e == pl.num_programs(1) - 1)
    def _():
        y_ref[...] = h_ref[...] + acc_ref[...]


def _moe(hn, gates, h, consts, tm):
    n = hn.shape[0]
    row = lambda i, e: (i, 0)
    return pl.pallas_call(
        _moe_kernel,
        grid=(n // tm, N_EXPERTS),
        in_specs=[
            pl.BlockSpec((tm, D_MODEL), row),
            pl.BlockSpec((tm, LANES), row),
            pl.BlockSpec((tm, D_MODEL), row),
            pl.BlockSpec((1, D_MODEL, D_EXPERT), lambda i, e: (e, 0, 0)),
            pl.BlockSpec((1, D_MODEL, D_EXPERT), lambda i, e: (e, 0, 0)),
            pl.BlockSpec((1, D_EXPERT, D_MODEL), lambda i, e: (e, 0, 0)),
        ],
        out_specs=pl.BlockSpec((tm, D_MODEL), row),
        out_shape=jax.ShapeDtypeStruct((n, D_MODEL), F32),
        scratch_shapes=[pltpu.VMEM((tm, D_MODEL), F32)],
        compiler_params=pltpu.CompilerParams(dimension_semantics=("arbitrary", "arbitrary"),
                                             vmem_limit_bytes=VMEM_LIMIT),
        name="moe_experts",
    )(hn, gates, h, consts["w_g"], consts["w_u"], consts["w_d"])


def _rope_tables(positions):
    half = HEAD_DIM // 2
    inv = ROPE_THETA ** (-jnp.arange(half, dtype=F32) / half)
    ang = positions.astype(F32)[:, None] * inv[None, :]
    cos, sin = jnp.cos(ang), jnp.sin(ang)
    cos64 = jnp.concatenate([cos, cos], axis=1)
    sin64 = jnp.concatenate([-sin, sin], axis=1)
    return jnp.tile(cos64, (1, LANES // HEAD_DIM)), jnp.tile(sin64, (1, LANES // HEAD_DIM))


def _positions_last(cache):
    nd = cache.ndim
    t = jnp.transpose(cache, (0, 1) + tuple(range(3, nd)) + (2,))
    return t.reshape(t.shape[:2] + (-1, PAGE))


def _positions_first(t, feature_shape):
    b, _, s = t.shape
    nf = len(feature_shape)
    t = t.reshape((b,) + tuple(feature_shape) + (s,))
    return jnp.transpose(t, (0, nf + 1) + tuple(range(1, nf + 1)))[None]


def _tile_lanes(v):
    return jnp.tile(v.astype(F32).reshape(1, -1), (1, LANES // v.shape[-1]))


def kernel(x_prompt, x_sample, cache_a_k, cache_a_v, cache_idx_k, cache_b_k, cache_b_v, page_table,
           g_mix, w_in, q_norm_a, k_norm_a, q_norm_b, k_norm_b, lambda_q1, lambda_k1, lambda_q2, lambda_k2,
           subln_b, w_out, g_ffn, w_router_group, b_router_group, w_router_expert, b_router_expert,
           w_exp_gate, w_exp_up, w_exp_down):
    depth = w_in.shape[0]
    assert depth == 1, "single-layer stack"
    batch, seq, _ = x_prompt.shape
    nb, n_new, _ = x_sample.shape
    n_pages = page_table.shape[1]
    past = n_pages * PAGE
    l = 0
    lam_init = 0.8 - 0.6 * math.exp(-0.3 * l)

    wide = N_SEC * SEC
    w = w_in[l]
    w_ik = w[:, wide:wide + HEAD_DIM]
    w_iw = w[:, wide + HEAD_DIM:]
    w_tail = jnp.concatenate(
        [w_ik, w_ik, w_iw, jnp.zeros((D_MODEL, LANES - w_iw.shape[1]), F32)], axis=1).astype(BF16)
    gmat = (np.arange(LANES)[:, None] // HEAD_DIM == np.arange(LANES)[None, :] // HEAD_DIM)
    w_r = jnp.concatenate([w_router_group[l], w_router_expert[l],
                           jnp.zeros((D_MODEL, LANES - N_GROUPS - N_EXPERTS), F32)], axis=1)
    w_r_hi = w_r.astype(BF16)
    b_r = jnp.concatenate([b_router_group[l], b_router_expert[l],
                           jnp.zeros((LANES - N_GROUPS - N_EXPERTS,), F32)]).reshape(1, LANES)
    consts = {
        "g_mix": g_mix[l].reshape(1, D_MODEL),
        "w_main": w[:, :wide].astype(BF16),
        "w_tail": w_tail,
        "gains": jnp.concatenate([_tile_lanes(q_norm_a[l]), _tile_lanes(k_norm_a[l]),
                                  _tile_lanes(q_norm_b[l]), _tile_lanes(k_norm_b[l])], axis=0),
        "gmat": jnp.asarray(gmat, BF16),
        "w_out_a": w_out[l][:SEC].astype(BF16),
        "w_out_b": w_out[l][SEC:].astype(BF16),
        "g_ffn": g_ffn[l].reshape(1, D_MODEL),
        "w_r_hi": w_r_hi,
        "w_r_lo": (w_r - w_r_hi.astype(F32)).astype(BF16),
        "b_r": b_r,
        "w_g": w_exp_gate[l].astype(BF16),
        "w_u": w_exp_up[l].astype(BF16),
        "w_d": w_exp_down[l].astype(BF16),
    }
    lam_vecs = jnp.stack([lambda_q1[l], lambda_k1[l], lambda_q2[l], lambda_k2[l]]).astype(F32)
    subln = subln_b[l].reshape(1, LANES)

    tm_p = 256
    cos_p, sin_p = _rope_tables(jnp.arange(seq))
    xp = x_prompt.reshape(batch * seq, D_MODEL)
    (akt, avt, bkt, bv4, ikt, iwf, aqb, aktb, avb, bqb, bktb, bvb, iqb, iktb) = _project(
        xp, dict(consts, cos=cos_p, sin=sin_p), tm_p, seq, True)
    ones = jnp.ones((past + PAGE, LANES), BF16)
    a_out = _dsa_prompt(iqb, iwf, aqb, iktb, aktb, avb, ones[:seq], batch, seq, 256, 8)
    b_out = _diff_prompt(bqb, bktb, bvb, lam_vecs, subln, batch, seq, 128, lam_init, 8)
    h_p, hn_p, gates_p = _merge(xp, a_out, b_out, consts, tm_p)
    y_p = _moe(hn_p, gates_p, h_p, consts, 1024)

    ns = nb * n_new
    tm_s = 256
    cos_s, sin_s = _rope_tables(past + (jnp.arange(tm_s) % n_new))
    xs = x_sample.reshape(ns, D_MODEL)
    (akf_s, avf_s, bkf_s, bvf_s, ikf_s, iwf_s, aqb_s, _, _, bqb_s, _, _, iqb_s, _) = _project(
        xs, dict(consts, cos=cos_s, sin=sin_s), tm_s, tm_s, False)
    score_s = _sample_scores(
        page_table,
        iqb_s.astype(F32).reshape(nb, n_new * 8, HEAD_DIM),
        iwf_s[:, :8].reshape(nb, n_new * 8, 1),
        ikf_s[:, :HEAD_DIM].reshape(nb, n_new, HEAD_DIM),
        _positions_last(cache_idx_k))
    sel_s = _topk_rows(score_s.reshape(ns, past + PAGE), ones, 256).reshape(nb, n_new, past + PAGE)
    a_out_s, b_out_s = _sample_attention(
        page_table, sel_s,
        aqb_s.astype(F32).reshape(nb, n_new, SEC),
        bqb_s.astype(F32).reshape(nb, n_new, SEC),
        akf_s.reshape(nb, n_new, SEC), avf_s.reshape(nb, n_new, SEC),
        bkf_s.reshape(nb, n_new, SEC), bvf_s.reshape(nb, n_new, SEC),
        lam_vecs, subln,
        _positions_last(cache_a_k), _positions_last(cache_a_v), _positions_last(cache_b_k),
        cache_b_v.reshape(cache_b_v.shape[:2] + (PAGE * (SEC // LANES), LANES)),
        lam_init)
    h_s, hn_s, gates_s = _merge(xs, a_out_s.reshape(ns, SEC), b_out_s.reshape(ns, SEC), consts, tm_s)
    y_s = _moe(hn_s, gates_s, h_s, consts, ns)

    a_heads = SEC // HEAD_DIM
    b_heads = SEC // LANES
    return (
        y_p.reshape(batch, seq, D_MODEL),
        y_s.reshape(nb, n_new, D_MODEL),
        _positions_first(akt, (a_heads, HEAD_DIM)),
        _positions_first(avt, (a_heads, HEAD_DIM)),
        _positions_first(ikt[:, :HEAD_DIM, :], (HEAD_DIM,)),
        _positions_first(bkt, (b_heads, 2, HEAD_DIM)),
        bv4.reshape(1, batch, seq, b_heads, LANES),
        akf_s.reshape(1, nb, n_new, a_heads, HEAD_DIM),
        avf_s.reshape(1, nb, n_new, a_heads, HEAD_DIM),
        ikf_s[:, :HEAD_DIM].reshape(1, nb, n_new, HEAD_DIM),
        bkf_s.reshape(1, nb, n_new, b_heads, 2, HEAD_DIM),
        bvf_s.reshape(1, nb, n_new, b_heads, LANES),
    )
```
